```python
import jax
import jax.numpy as jnp
from jax import lax
import numpy as np

D_MODEL = 4096
BATCH = 2
SEQ = 8192
DEPTH = 2

MIX_WIDTH = D_MODEL
POOL_WIDTH = MIX_WIDTH // 2
POOL_WINDOWS = (2, 4, 8, 16)
N_POOL_GROUPS = len(POOL_WINDOWS)
POOL_GROUP_DIM = POOL_WIDTH // N_POOL_GROUPS
ATT_WIDTH = MIX_WIDTH - POOL_WIDTH
HEAD_DIM = 128
N_ATT_HEADS = ATT_WIDTH // HEAD_DIM
Q_BLOCK = 128
ATT_SCALE = HEAD_DIM ** -0.5
PROJ_WIDTH = POOL_WIDTH + 3 * ATT_WIDTH + N_ATT_HEADS
N_EXPERTS = 32
TOP_K = 4
EXPERT_DIM = D_MODEL // 8
SWIGLU_ALPHA = 1.702
SWIGLU_LIMIT = 7.0
MOE_BLOCK = 128
N_MOD = 6
NORM_EPS = 1e-6

kernel_name = 'hybrid_pool_fox_moe_adaln'


def rms_norm(x, gain):
    x32 = x.astype(jnp.float32)
    y = x32 * lax.rsqrt(jnp.mean(x32 * x32, axis=-1, keepdims=True) + NORM_EPS)
    return (y * gain.astype(jnp.float32)).astype(x.dtype)


def multiscale_pool(u, pool_w, pool_scale):
    b, s, _ = u.shape
    ug = u.reshape(b, s, N_POOL_GROUPS, POOL_GROUP_DIM).astype(jnp.float32)
    cs = jnp.cumsum(ug, axis=1)
    max_w = max(POOL_WINDOWS)
    cs_pad = jnp.pad(cs, ((0, 0), (max_w, 0), (0, 0), (0, 0)))
    pos = jnp.arange(s)
    means = []
    for g, w in enumerate(POOL_WINDOWS):
        lagged = cs_pad[:, max_w - w:max_w - w + s, g]
        count = jnp.minimum(pos + 1, w).astype(jnp.float32)[None, :, None]
        means.append((cs[:, :, g] - lagged) / count)
    mixed = jnp.stack(means, axis=2) - ug
    out = jnp.einsum('bsgc,gcd->bsgd', mixed.astype(u.dtype), pool_w,
                     preferred_element_type=jnp.float32)
    out = out * pool_scale.reshape(N_POOL_GROUPS, POOL_GROUP_DIM).astype(jnp.float32)
    return out.reshape(b, s, POOL_WIDTH).astype(u.dtype)


def forgetting_attention(q, k, v, log_f):
    b, s, h, dh = q.shape
    nb = s // Q_BLOCK
    cum = jnp.cumsum(log_f, axis=1)
    cum_k = jnp.transpose(cum, (0, 2, 1))
    q_blocks = jnp.transpose(q.reshape(b, nb, Q_BLOCK, h, dh), (1, 0, 2, 3, 4))
    c_blocks = jnp.transpose(cum.reshape(b, nb, Q_BLOCK, h), (1, 0, 3, 2))
    k_pos = jnp.arange(s)

    def one_block(args):
        qi, ci, i = args
        scores = jnp.einsum('bqhd,bkhd->bhqk', qi, k, preferred_element_type=jnp.float32) * ATT_SCALE
        scores = scores + ci[..., None] - cum_k[:, :, None, :]
        q_pos = i * Q_BLOCK + jnp.arange(Q_BLOCK)
        causal = k_pos[None, :] <= q_pos[:, None]
        scores = jnp.where(causal[None, None], scores, -jnp.inf)
        probs = jax.nn.softmax(scores, axis=-1)
        return jnp.einsum('bhqk,bkhd->bqhd', probs.astype(v.dtype), v,
                          preferred_element_type=jnp.float32).astype(v.dtype)

    out = lax.map(one_block, (q_blocks, c_blocks, jnp.arange(nb)))
    return jnp.transpose(out, (1, 0, 2, 3, 4)).reshape(b, s, h, dh)


def clamped_swiglu_expert(xb, w1, b1, w2, b2):
    hid = jnp.einsum('td,df->tf', xb, w1, preferred_element_type=jnp.float32) + b1.astype(jnp.float32)
    glu = jnp.minimum(hid[:, :EXPERT_DIM], SWIGLU_LIMIT)
    lin = jnp.clip(hid[:, EXPERT_DIM:], -SWIGLU_LIMIT, SWIGLU_LIMIT)
    act = glu * jax.nn.sigmoid(SWIGLU_ALPHA * glu) * (lin + 1.0)
    out = jnp.einsum('tf,fd->td', act.astype(xb.dtype), w2, preferred_element_type=jnp.float32)
    return out + b2.astype(jnp.float32)


def moe_ffn(h, router_w, router_b, w1, b1, w2, b2):
    b, s, d = h.shape
    n_tok = b * s
    tokens = h.reshape(n_tok, d)
    logits = jnp.einsum('nd,de->ne', tokens, router_w, preferred_element_type=jnp.float32)
    logits = logits + router_b.astype(jnp.float32)
    top_logit, top_idx = lax.top_k(logits, TOP_K)
    top_w = jax.nn.softmax(top_logit, axis=-1)
    n_assign = n_tok * TOP_K
    flat_e = top_idx.reshape(-1).astype(jnp.int32)
    order = jnp.argsort(flat_e)
    sorted_e = flat_e[order]
    sorted_tok = (order // TOP_K).astype(jnp.int32)
    sorted_w = top_w.reshape(-1)[order]
    counts = jnp.bincount(flat_e, length=N_EXPERTS).astype(jnp.int32)
    padded = (counts + MOE_BLOCK - 1) // MOE_BLOCK * MOE_BLOCK
    pad_end = jnp.cumsum(padded)
    pad_start = pad_end - padded
    grp_start = jnp.cumsum(counts) - counts
    dest = pad_start[sorted_e] + jnp.arange(n_assign, dtype=jnp.int32) - grp_start[sorted_e]
    n_blocks = -(-n_assign // MOE_BLOCK) + N_EXPERTS
    n_rows = n_blocks * MOE_BLOCK
    row_tok = jnp.full((n_rows,), n_tok, jnp.int32).at[dest].set(sorted_tok)
    row_w = jnp.zeros((n_rows,), jnp.float32).at[dest].set(sorted_w)
    block_e = jnp.minimum(
        jnp.searchsorted(pad_end, jnp.arange(n_blocks, dtype=jnp.int32) * MOE_BLOCK, side='right'),
        N_EXPERTS - 1)
    tokens_pad = jnp.concatenate([tokens, jnp.zeros((1, d), tokens.dtype)], axis=0)

    def step(acc, blk):
        rows, wts, e = blk
        y = clamped_swiglu_expert(tokens_pad[rows], w1[e], b1[e], w2[e], b2[e])
        return acc.at[rows].add(y * wts[:, None]), None

    acc0 = jnp.zeros((n_tok + 1, d), jnp.float32)
    acc, _ = lax.scan(step, acc0, (row_tok.reshape(n_blocks, MOE_BLOCK),
                                   row_w.reshape(n_blocks, MOE_BLOCK), block_e))
    return acc[:n_tok].reshape(b, s, d).astype(h.dtype)


def hybrid_layer(x, c_act, ada_w, ada_b, norm1_g, w_in, forget_b, pool_w, pool_scale,
                 q_norm_g, k_norm_g, w_out, norm2_g, router_w, router_b,
                 expert_w1, expert_b1, expert_w2, expert_b2):
    b, s, _ = x.shape
    mod = (jnp.einsum('bd,de->be', c_act, ada_w, preferred_element_type=jnp.float32)
           + ada_b.astype(jnp.float32))[:, None, :]
    shift1, scale1, gate1, shift2, scale2, gate2 = jnp.split(mod, N_MOD, axis=-1)

    h = (rms_norm(x, norm1_g).astype(jnp.float32) * (1.0 + scale1) + shift1).astype(x.dtype)
    proj = jnp.einsum('bsd,dp->bsp', h, w_in, preferred_element_type=jnp.float32).astype(x.dtype)
    o = POOL_WIDTH
    u = proj[..., :o]
    q = proj[..., o:o + ATT_WIDTH].reshape(b, s, N_ATT_HEADS, HEAD_DIM)
    k = proj[..., o + ATT_WIDTH:o + 2 * ATT_WIDTH].reshape(b, s, N_ATT_HEADS, HEAD_DIM)
    v = proj[..., o + 2 * ATT_WIDTH:o + 3 * ATT_WIDTH].reshape(b, s, N_ATT_HEADS, HEAD_DIM)
    f_logit = proj[..., o + 3 * ATT_WIDTH:]
    pool_out = multiscale_pool(u, pool_w, pool_scale)
    q = rms_norm(q, q_norm_g)
    k = rms_norm(k, k_norm_g)
    log_f = jax.nn.log_sigmoid(f_logit.astype(jnp.float32) + forget_b.astype(jnp.float32))
    att_out = forgetting_attention(q, k, v, log_f).reshape(b, s, ATT_WIDTH)
    mixed = jnp.concatenate([pool_out, att_out], axis=-1)
    mix_out = jnp.einsum('bsm,md->bsd', mixed, w_out, preferred_element_type=jnp.float32)
    x = (x.astype(jnp.float32) + gate1 * mix_out).astype(x.dtype)

    h2 = (rms_norm(x, norm2_g).astype(jnp.float32) * (1.0 + scale2) + shift2).astype(x.dtype)
    ffn_out = moe_ffn(h2, router_w, router_b, expert_w1, expert_b1, expert_w2, expert_b2)
    return (x.astype(jnp.float32) + gate2 * ffn_out.astype(jnp.float32)).astype(x.dtype)


def setup_inputs(seed: int = 0) -> dict:
    key = jax.random.key(seed)
    ks = jax.random.split(key, 19)
    L, D = DEPTH, D_MODEL

    def nrm(k, shape, scale):
        return jax.random.normal(k, shape, jnp.float32) * scale

    return {
        'x': nrm(ks[0], (BATCH, SEQ, D), 1.0),
        'c': nrm(ks[1], (BATCH, D), 1.0),
        'ada_w': nrm(ks[2], (L, D, N_MOD * D), 0.5 * D ** -0.5),
        'ada_b': nrm(ks[3], (L, N_MOD * D), 0.02),
        'norm1_g': 1.0 + nrm(ks[4], (L, D), 0.05),
        'w_in': nrm(ks[5], (L, D, PROJ_WIDTH), D ** -0.5),
        'forget_b': 2.0 + nrm(ks[6], (L, N_ATT_HEADS), 0.5),
        'pool_w': nrm(ks[7], (L, N_POOL_GROUPS, POOL_GROUP_DIM, POOL_GROUP_DIM), POOL_GROUP_DIM ** -0.5),
        'pool_scale': 1.0 + nrm(ks[8], (L, POOL_WIDTH), 0.1),
        'q_norm_g': 1.0 + nrm(ks[9], (L, HEAD_DIM), 0.05),
        'k_norm_g': 1.0 + nrm(ks[10], (L, HEAD_DIM), 0.05),
        'w_out': nrm(ks[11], (L, MIX_WIDTH, D), MIX_WIDTH ** -0.5),
        'norm2_g': 1.0 + nrm(ks[12], (L, D), 0.05),
        'router_w': nrm(ks[13], (L, D, N_EXPERTS), D ** -0.5),
        'router_b': nrm(ks[14], (L, N_EXPERTS), 0.01),
        'expert_w1': nrm(ks[15], (L, N_EXPERTS, D, 2 * EXPERT_DIM), D ** -0.5),
        'expert_b1': nrm(ks[16], (L, N_EXPERTS, 2 * EXPERT_DIM), 0.01),
        'expert_w2': nrm(ks[17], (L, N_EXPERTS, EXPERT_DIM, D), EXPERT_DIM ** -0.5),
        'expert_b2': nrm(ks[18], (L, N_EXPERTS, D), 0.01),
    }


def reference(x, c, ada_w, ada_b, norm1_g, w_in, forget_b, pool_w, pool_scale,
              q_norm_g, k_norm_g, w_out, norm2_g, router_w, router_b,
              expert_w1, expert_b1, expert_w2, expert_b2):
    c_act = jax.nn.silu(c)
    for l in range(DEPTH):
        x = hybrid_layer(x, c_act, ada_w[l], ada_b[l], norm1_g[l], w_in[l], forget_b[l],
                         pool_w[l], pool_scale[l], q_norm_g[l], k_norm_g[l], w_out[l],
                         norm2_g[l], router_w[l], router_b[l], expert_w1[l], expert_b1[l],
                         expert_w2[l], expert_b2[l])
    return x
```

```python
import functools

import jax
import jax.numpy as jnp
from jax import lax
from jax.experimental import pallas as pl
from jax.experimental.pallas import tpu as pltpu

HEAD_DIM = 128
POOL_WINDOWS = (2, 4, 8, 16)
TOP_K = 4
N_MOD = 6
NORM_EPS = 1e-6
SWIGLU_ALPHA = 1.702
SWIGLU_LIMIT = 7.0

LANES = 128
SUBLANES = 8
BF16_ROWS = 16
POOL_HALO = 16
VMEM_LIMIT_BYTES = 56 * 2**20

F32 = jnp.float32
BF16 = jnp.bfloat16


def _cparams(*sem):
    return pltpu.CompilerParams(dimension_semantics=sem, vmem_limit_bytes=VMEM_LIMIT_BYTES)


def _tiles(n_tok, seq, d_model):
    return dict(
        mod_tn=min(512, d_model),
        in_tm=min(512, seq),
        norm_rc=min(64, seq),
        att_tq=min(512, seq),
        out_tm=min(1024, seq),
        out_tn=min(512, d_model),
        rt_tm=min(512, seq),
        moe_bm=min(256, n_tok),
        cmb_tm=min(128, seq),
    )


def _mod_kernel(c_ref, w_ref, b_ref, o_ref):
    c = c_ref[...]
    ca = c / (1.0 + jnp.exp(-c))
    w = w_ref[0].astype(BF16)
    o_ref[0] = jnp.dot(ca.astype(BF16), w, preferred_element_type=F32) + b_ref[0]


def _modulation(c, ada_w, ada_b, tn):
    depth, d_model, n_out = ada_w.shape
    b = c.shape[0]
    c_pad = jnp.zeros((SUBLANES, d_model), F32).at[:b].set(c)
    out = pl.pallas_call(
        _mod_kernel,
        grid=(depth, n_out // tn),
        in_specs=[
            pl.BlockSpec((SUBLANES, d_model), lambda l, j: (0, 0)),
            pl.BlockSpec((1, d_model, tn), lambda l, j: (l, 0, j)),
            pl.BlockSpec((1, 1, tn), lambda l, j: (l, 0, j)),
        ],
        out_specs=pl.BlockSpec((1, SUBLANES, tn), lambda l, j: (l, 0, j)),
        out_shape=jax.ShapeDtypeStruct((depth, SUBLANES, n_out), F32),
        compiler_params=_cparams("arbitrary", "arbitrary"),
        name="adaln_mod",
    )(c_pad, ada_w, ada_b.reshape(depth, 1, n_out))
    return out[:, :b]


def _in_kernel(x_ref, g_ref, sc_ref, sh_ref, w_ref, wf_ref, fb_ref, pw_ref, ps_ref, qg_ref, kg_ref,
               main_ref, cum_ref, h_sc, tail_sc, fc_sc, *, tm, rc, tpb, gd):
    i = pl.program_id(0)
    j = pl.program_id(1)
    first = (i % tpb) == 0

    @pl.when(j == 0)
    def _norm():
        def body(c, carry):
            r0 = pl.multiple_of(c * rc, rc)
            x = x_ref[pl.ds(r0, rc), :]
            ms = jnp.mean(x * x, axis=-1, keepdims=True)
            y = x * lax.rsqrt(ms + NORM_EPS)
            h = (y * g_ref[...]) * (1.0 + sc_ref[0]) + sh_ref[0]
            h_sc[pl.ds(r0, rc), :] = h.astype(BF16)
            return carry
        lax.fori_loop(0, tm // rc, body, 0)

    def proj():
        return jnp.dot(h_sc[...], w_ref[...], preferred_element_type=F32)

    for g, win in enumerate(POOL_WINDOWS):
        @pl.when(j == g)
        def _pool(g=g, win=win):
            u = proj()
            prev = jnp.where(first, 0.0, tail_sc[g])
            s = jnp.concatenate([prev, u], axis=0)
            shift = 1
            while shift < win:
                s = s + pltpu.roll(s, shift, 0)
                shift *= 2
            wsum = s[POOL_HALO:]
            pos = (i % tpb) * tm + lax.broadcasted_iota(jnp.int32, (tm, 1), 0)
            cnt = jnp.minimum(pos + 1, win).astype(F32)
            mixed = wsum / cnt - u
            po = jnp.dot(mixed.astype(BF16), pw_ref[0], preferred_element_type=F32) * ps_ref[0]
            main_ref[...] = po.astype(BF16)
            tail_sc[g] = u[tm - POOL_HALO:]

    def qk_norm(gain_ref):
        r = proj()
        for c in range(gd // HEAD_DIM):
            sl = slice(c * HEAD_DIM, (c + 1) * HEAD_DIM)
            rc_ = r[:, sl]
            ms = jnp.mean(rc_ * rc_, axis=-1, keepdims=True)
            main_ref[:, sl] = ((rc_ * lax.rsqrt(ms + NORM_EPS)) * gain_ref[:, sl]).astype(BF16)

    @pl.when((j >= 4) & (j < 8))
    def _q():
        qk_norm(qg_ref)

    @pl.when((j >= 8) & (j < 12))
    def _k():
        qk_norm(kg_ref)

    @pl.when((j >= 12) & (j < 16))
    def _v():
        main_ref[...] = proj().astype(BF16)

    @pl.when(j == 16)
    def _forget():
        z = lax.dot_general(wf_ref[...], h_sc[...], (((1,), (1,)), ((), ())),
                            preferred_element_type=F32) + fb_ref[...]
        ls = jnp.minimum(z, 0.0) - jnp.log1p(jnp.exp(-jnp.abs(z)))
        p0 = ls.astype(BF16)
        r1 = ls - p0.astype(F32)
        p1 = r1.astype(BF16)
        p2 = (r1 - p1.astype(F32)).astype(BF16)
        rr = lax.broadcasted_iota(jnp.int32, (tm, tm), 0)
        cc = lax.broadcasted_iota(jnp.int32, (tm, tm), 1)
        tri = jnp.where(rr <= cc, 1.0, 0.0).astype(BF16)
        cum = (jnp.dot(p0, tri, preferred_element_type=F32)
               + jnp.dot(p1, tri, preferred_element_type=F32)
               + jnp.dot(p2, tri, preferred_element_type=F32))
        cum = cum + jnp.where(first, 0.0, fc_sc[:, 0:1])
        cum_ref[0] = cum
        fc_sc[...] = jnp.broadcast_to(cum[:, tm - 1:tm], fc_sc.shape)


def _in_projection(x2d, gain, scale, shift, w_main, wf_t, fb, pool_w, pool_scale, qg, kg, *, seq, t):
    n_tok, d_model = x2d.shape
    batch = n_tok // seq
    tm, rc = t["in_tm"], t["norm_rc"]
    gd = pool_w.shape[-1]
    hp = wf_t.shape[0]
    n_main = w_main.shape[1]
    tpb = seq // tm
    nj = n_main // gd + 1
    last = n_main // gd - 1
    kern = functools.partial(_in_kernel, tm=tm, rc=rc, tpb=tpb, gd=gd)
    return pl.pallas_call(
        kern,
        grid=(n_tok // tm, nj),
        in_specs=[
            pl.BlockSpec((tm, d_model), lambda i, j: (i, 0)),
            pl.BlockSpec((1, d_model), lambda i, j: (0, 0)),
            pl.BlockSpec((1, 1, d_model), lambda i, j: (i // tpb, 0, 0)),
            pl.BlockSpec((1, 1, d_model), lambda i, j: (i // tpb, 0, 0)),
            pl.BlockSpec((d_model, gd), lambda i, j: (0, jnp.minimum(j, last))),
            pl.BlockSpec((hp, d_model), lambda i, j: (0, 0)),
            pl.BlockSpec((hp, 1), lambda i, j: (0, 0)),
            pl.BlockSpec((1, gd, gd), lambda i, j: (jnp.minimum(j, 3), 0, 0)),
            pl.BlockSpec((1, 1, gd), lambda i, j: (jnp.minimum(j, 3), 0, 0)),
            pl.BlockSpec((1, gd), lambda i, j: (0, 0)),
            pl.BlockSpec((1, gd), lambda i, j: (0, 0)),
        ],
        out_specs=[
            pl.BlockSpec((tm, gd), lambda i, j: (i, jnp.minimum(j, last))),
            pl.BlockSpec((1, hp, tm), lambda i, j: (i // tpb, 0, i % tpb)),
        ],
        out_shape=[
            jax.ShapeDtypeStruct((n_tok, n_main), BF16),
            jax.ShapeDtypeStruct((batch, hp, seq), F32),
        ],
        scratch_shapes=[
            pltpu.VMEM((tm, d_model), BF16),
            pltpu.VMEM((len(POOL_WINDOWS), POOL_HALO, gd), F32),
            pltpu.VMEM((hp, LANES), F32),
        ],
        compiler_params=_cparams("arbitrary", "arbitrary"),
        name="norm_in_proj",
    )(x2d, gain, scale, shift, w_main, wf_t, fb, pool_w, pool_scale, qg, kg)


def _attn_kernel(q_ref, k_ref, v_ref, c_ref, o_ref, m_sc, l_sc, acc_sc, *, tq, scale):
    qi = pl.program_id(2)
    q = q_ref[...]
    m_sc[...] = jnp.full(m_sc.shape, -jnp.inf, F32)
    l_sc[...] = jnp.zeros(l_sc.shape, F32)
    acc_sc[...] = jnp.zeros(acc_sc.shape, F32)

    def block(kj, diagonal):
        start = pl.multiple_of(kj * tq, tq)
        k = k_ref[pl.ds(start, tq), :]
        v = v_ref[pl.ds(start, tq), :]
        s = lax.dot_general(q, k, (((1,), (1,)), ((), ())), preferred_element_type=F32)
        s = s * scale - c_ref[0, :, pl.ds(start, tq)]
        if diagonal:
            rr = lax.broadcasted_iota(jnp.int32, (tq, tq), 0)
            cc = lax.broadcasted_iota(jnp.int32, (tq, tq), 1)
            s = jnp.where(cc <= rr, s, -jnp.inf)
        m_prev = m_sc[...]
        m_new = jnp.maximum(m_prev, jnp.max(s, axis=-1, keepdims=True))
        p = jnp.exp(s - m_new)
        alpha = jnp.exp(m_prev - m_new)
        l_sc[...] = alpha * l_sc[...] + jnp.sum(p, axis=-1, keepdims=True)
        acc_sc[...] = alpha * acc_sc[...] + jnp.dot(p.astype(BF16), v, preferred_element_type=F32)
        m_sc[...] = m_new

    def body(kj, carry):
        block(kj, False)
        return carry

    lax.fori_loop(0, qi, body, 0)
    block(qi, True)
    o_ref[...] = (acc_sc[...] / l_sc[...]).astype(o_ref.dtype)


def _attention(main, cum, *, seq, n_heads, q_off, k_off, v_off, t):
    n_tok = main.shape[0]
    batch = n_tok // seq
    tq = t["att_tq"]
    nq = seq // tq
    hp = cum.shape[1]
    cum3 = cum.reshape(batch * hp, 1, seq)
    kern = functools.partial(_attn_kernel, tq=tq, scale=HEAD_DIM ** -0.5)
    qb, kb, vb = q_off // HEAD_DIM, k_off // HEAD_DIM, v_off // HEAD_DIM
    return pl.pallas_call(
        kern,
        grid=(batch, n_heads, nq),
        in_specs=[
            pl.BlockSpec((tq, HEAD_DIM), lambda b, h, i: (b * nq + i, qb + h)),
            pl.BlockSpec((seq, HEAD_DIM), lambda b, h, i: (b, kb + h)),
            pl.BlockSpec((seq, HEAD_DIM), lambda b, h, i: (b, vb + h)),
            pl.BlockSpec((1, 1, seq), lambda b, h, i: (b * hp + h, 0, 0)),
        ],
        out_specs=pl.BlockSpec((tq, HEAD_DIM), lambda b, h, i: (b * nq + i, h)),
        out_shape=jax.ShapeDtypeStruct((n_tok, n_heads * HEAD_DIM), BF16),
        scratch_shapes=[
            pltpu.VMEM((tq, 1), F32),
            pltpu.VMEM((tq, 1), F32),
            pltpu.VMEM((tq, HEAD_DIM), F32),
        ],
        compiler_params=_cparams("arbitrary", "arbitrary", "arbitrary"),
        name="forget_attention",
    )(main, main, main, cum3)


def _out_kernel(po_ref, at_ref, wp_ref, wa_ref, x_ref, g_ref, o_ref):
    mix = (jnp.dot(po_ref[...], wp_ref[...], preferred_element_type=F32)
           + jnp.dot(at_ref[...], wa_ref[...], preferred_element_type=F32))
    o_ref[...] = x_ref[...] + g_ref[0] * mix


def _out_projection(main, att, w_out, x2d, gate, *, seq, pool_width, t):
    n_tok, d_model = x2d.shape
    tm, tn = t["out_tm"], t["out_tn"]
    att_width = att.shape[1]
    tpb = seq // tm
    return pl.pallas_call(
        _out_kernel,
        grid=(n_tok // tm, d_model // tn),
        in_specs=[
            pl.BlockSpec((tm, pool_width), lambda i, j: (i, 0)),
            pl.BlockSpec((tm, att_width), lambda i, j: (i, 0)),
            pl.BlockSpec((pool_width, tn), lambda i, j: (0, j)),
            pl.BlockSpec((att_width, tn), lambda i, j: (pool_width // att_width, j)),
            pl.BlockSpec((tm, tn), lambda i, j: (i, j)),
            pl.BlockSpec((1, 1, tn), lambda i, j: (i // tpb, 0, j)),
        ],
        out_specs=pl.BlockSpec((tm, tn), lambda i, j: (i, j)),
        out_shape=jax.ShapeDtypeStruct((n_tok, d_model), F32),
        compiler_params=_cparams("arbitrary", "arbitrary"),
        name="out_proj_residual",
    )(main, att, w_out, w_out, x2d, gate)


def _router_kernel(x_ref, g_ref, sc_ref, sh_ref, rw_ref, rb_ref,
                   hp_ref, idx_ref, wts_ref, rank_ref, cnt_ref, h_sc, cnt_sc, *, tm, rc, half):
    i = pl.program_id(0)

    @pl.when(i == 0)
    def _init():
        cnt_sc[...] = jnp.zeros(cnt_sc.shape, F32)

    def body(c, carry):
        r0 = pl.multiple_of(c * rc, rc)
        x = x_ref[pl.ds(r0, rc), :]
        ms = jnp.mean(x * x, axis=-1, keepdims=True)
        y = x * lax.rsqrt(ms + NORM_EPS)
        h = ((y * g_ref[...]) * (1.0 + sc_ref[0]) + sh_ref[0]).astype(BF16)
        h_sc[pl.ds(r0, rc), :] = h
        bits = lax.bitcast_convert_type(h.astype(F32), jnp.uint32)
        lo = lax.shift_right_logical(bits[:, :half], jnp.uint32(16))
        hi = bits[:, half:] & jnp.uint32(0xFFFF0000)
        hp_ref[pl.ds(r0, rc), :] = lo | hi
        return carry
    lax.fori_loop(0, tm // rc, body, 0)

    logits = jnp.dot(h_sc[...], rw_ref[...], preferred_element_type=F32) + rb_ref[...]
    lane = lax.broadcasted_iota(jnp.int32, logits.shape, 1).astype(F32)
    vals, sels, hots = [], [], []
    cur = logits
    for _ in range(TOP_K):
        mx = jnp.max(cur, axis=-1, keepdims=True)
        sel = jnp.min(jnp.where(cur == mx, lane, float(LANES)), axis=-1, keepdims=True)
        hot = lane == sel
        vals.append(mx)
        sels.append(sel)
        hots.append(hot)
        cur = jnp.where(hot, -jnp.inf, cur)
    exps = [jnp.exp(v - vals[0]) for v in vals]
    denom = exps[0] + exps[1] + exps[2] + exps[3]

    hot_sum = jnp.zeros(logits.shape, F32)
    for hot in hots:
        hot_sum = hot_sum + jnp.where(hot, 1.0, 0.0)
    rr = lax.broadcasted_iota(jnp.int32, (tm, tm), 0)
    cc = lax.broadcasted_iota(jnp.int32, (tm, tm), 1)
    below = jnp.where(cc < rr, 1.0, 0.0).astype(BF16)
    base = jnp.dot(below, hot_sum.astype(BF16), preferred_element_type=F32) + cnt_sc[...]

    idx_out = jnp.zeros(logits.shape, F32)
    wts_out = jnp.zeros(logits.shape, F32)
    rank_out = jnp.zeros(logits.shape, F32)
    for k in range(TOP_K):
        col = lane == float(k)
        rank_k = jnp.sum(jnp.where(hots[k], base, 0.0), axis=-1, keepdims=True)
        idx_out = jnp.where(col, sels[k], idx_out)
        wts_out = jnp.where(col, exps[k] / denom, wts_out)
        rank_out = jnp.where(col, rank_k, rank_out)
    idx_ref[...] = idx_out.astype(jnp.int32)
    wts_ref[...] = wts_out
    rank_ref[...] = rank_out.astype(jnp.int32)
    total = cnt_sc[...] + jnp.sum(hot_sum, axis=0, keepdims=True)
    cnt_sc[...] = total
    cnt_ref[...] = total


def _router(x2d, gain, scale, shift, rw, rb, *, seq, t):
    n_tok, d_model = x2d.shape
    tm, rc = t["rt_tm"], t["norm_rc"]
    tpb = seq // tm
    half = d_model // 2
    kern = functools.partial(_router_kernel, tm=tm, rc=rc, half=half)
    return pl.pallas_call(
        kern,
        grid=(n_tok // tm,),
        in_specs=[
            pl.BlockSpec((tm, d_model), lambda i: (i, 0)),
            pl.BlockSpec((1, d_model), lambda i: (0, 0)),
            pl.BlockSpec((1, 1, d_model), lambda i: (i // tpb, 0, 0)),
            pl.BlockSpec((1, 1, d_model), lambda i: (i // tpb, 0, 0)),
            pl.BlockSpec((d_model, LANES), lambda i: (0, 0)),
            pl.BlockSpec((1, LANES), lambda i: (0, 0)),
        ],
        out_specs=[
            pl.BlockSpec((tm, half), lambda i: (i, 0)),
            pl.BlockSpec((tm, LANES), lambda i: (i, 0)),
            pl.BlockSpec((tm, LANES), lambda i: (i, 0)),
            pl.BlockSpec((tm, LANES), lambda i: (i, 0)),
            pl.BlockSpec((1, LANES), lambda i: (0, 0)),
        ],
        out_shape=[
            jax.ShapeDtypeStruct((n_tok, half), jnp.uint32),
            jax.ShapeDtypeStruct((n_tok, LANES), jnp.int32),
            jax.ShapeDtypeStruct((n_tok, LANES), F32),
            jax.ShapeDtypeStruct((n_tok, LANES), jnp.int32),
            jax.ShapeDtypeStruct((1, LANES), F32),
        ],
        scratch_shapes=[pltpu.VMEM((tm, d_model), BF16), pltpu.VMEM((1, LANES), F32)],
        compiler_params=_cparams("arbitrary"),
        name="norm_router_topk",
    )(x2d, gain, scale, shift, rw, rb)


def _gather_kernel(tok_ref, src_ref, o_ref, sem, *, bm):
    def row_copy(r, tok):
        return pltpu.make_async_copy(src_ref.at[pl.ds(tok, 1), :], o_ref.at[pl.ds(r, 1), :], sem)

    def issue(r, carry):
        row_copy(r, tok_ref[r]).start()
        return carry
    lax.fori_loop(0, bm, issue, 0)

    def drain(r, carry):
        row_copy(r, tok_ref[r]).wait()
        return carry
    lax.fori_loop(0, bm, drain, 0)


def _gather_rows(row_tok, packed, *, bm):
    n_rows = row_tok.shape[0]
    width = packed.shape[1]
    return pl.pallas_call(
        functools.partial(_gather_kernel, bm=bm),
        grid=(n_rows // bm,),
        in_specs=[
            pl.BlockSpec((bm,), lambda i: (i,), memory_space=pltpu.SMEM),
            pl.BlockSpec(memory_space=pl.ANY),
        ],
        out_specs=pl.BlockSpec((bm, width), lambda i: (i, 0)),
        out_shape=jax.ShapeDtypeStruct((n_rows, width), packed.dtype),
        scratch_shapes=[pltpu.SemaphoreType.DMA],
        compiler_params=_cparams("arbitrary"),
        name="moe_row_gather",
    )(row_tok, packed)


def _expert_kernel(be_ref, x_ref, w1_ref, b1_ref, w2_ref, b2_ref, o_ref, *, half, fdim):
    del be_ref
    words = x_ref[...]
    lo = lax.bitcast_convert_type(lax.shift_left(words, jnp.uint32(16)), F32).astype(BF16)
    hi = lax.bitcast_convert_type(words & jnp.uint32(0xFFFF0000), F32).astype(BF16)
    hid = (jnp.dot(lo, w1_ref[0, :half, :], preferred_element_type=F32)
           + jnp.dot(hi, w1_ref[0, half:, :], preferred_element_type=F32)) + b1_ref[0]
    glu = jnp.minimum(hid[:, :fdim], SWIGLU_LIMIT)
    lin = jnp.clip(hid[:, fdim:], -SWIGLU_LIMIT, SWIGLU_LIMIT)
    act = (glu / (1.0 + jnp.exp(-SWIGLU_ALPHA * glu))) * (lin + 1.0)
    o_ref[...] = jnp.dot(act.astype(BF16), w2_ref[0], preferred_element_type=F32) + b2_ref[0]


def _experts(block_e, xs, w1, b1, w2, b2, *, bm):
    n_rows, half = xs.shape
    n_exp, d_model, two_f = w1.shape
    fdim = two_f // 2
    grid_spec = pltpu.PrefetchScalarGridSpec(
        num_scalar_prefetch=1,
        grid=(n_rows // bm,),
        in_specs=[
            pl.BlockSpec((bm, half), lambda i, be: (i, 0)),
            pl.BlockSpec((1, d_model, two_f), lambda i, be: (be[i], 0, 0)),
            pl.BlockSpec((1, 1, two_f), lambda i, be: (be[i], 0, 0)),
            pl.BlockSpec((1, fdim, d_model), lambda i, be: (be[i], 0, 0)),
            pl.BlockSpec((1, 1, d_model), lambda i, be: (be[i], 0, 0)),
        ],
        out_specs=pl.BlockSpec((bm, d_model), lambda i, be: (i, 0)),
    )
    return pl.pallas_call(
        functools.partial(_expert_kernel, half=half, fdim=fdim),
        grid_spec=grid_spec,
        out_shape=jax.ShapeDtypeStruct((n_rows, d_model), F32),
        compiler_params=_cparams("arbitrary"),
        name="moe_experts",
    )(block_e, xs, w1, b1.reshape(n_exp, 1, two_f), w2, b2.reshape(n_exp, 1, d_model))


def _combine_kernel(dest_ref, y_ref, wts_ref, x_ref, g_ref, o_ref, buf, sem, *, tm):
    def row_copy(r, k):
        return pltpu.make_async_copy(y_ref.at[pl.ds(dest_ref[r * TOP_K + k], 1), :],
                                     buf.at[k, pl.ds(r, 1), :], sem)

    def issue(r, carry):
        for k in range(TOP_K):
            row_copy(r, k).start()
        return carry
    lax.fori_loop(0, tm, issue, 0)

    def drain(r, carry):
        for k in range(TOP_K):
            row_copy(r, k).wait()
        return carry
    lax.fori_loop(0, tm, drain, 0)

    w = wts_ref[...]
    acc = w[:, 0:1] * buf[0]
    for k in range(1, TOP_K):
        acc = acc + w[:, k:k + 1] * buf[k]
    o_ref[...] = x_ref[...] + g_ref[0] * acc


def _combine(dest, ys, wts, x2d, gate, *, seq, t):
    n_tok, d_model = x2d.shape
    tm = t["cmb_tm"]
    tpb = seq // tm
    return pl.pallas_call(
        functools.partial(_combine_kernel, tm=tm),
        grid=(n_tok // tm,),
        in_specs=[
            pl.BlockSpec((tm * TOP_K,), lambda i: (i,), memory_space=pltpu.SMEM),
            pl.BlockSpec(memory_space=pl.ANY),
            pl.BlockSpec((tm, LANES), lambda i: (i, 0)),
            pl.BlockSpec((tm, d_model), lambda i: (i, 0)),
            pl.BlockSpec((1, 1, d_model), lambda i: (i // tpb, 0, 0)),
        ],
        out_specs=pl.BlockSpec((tm, d_model), lambda i: (i, 0)),
        out_shape=jax.ShapeDtypeStruct((n_tok, d_model), F32),
        scratch_shapes=[pltpu.VMEM((TOP_K, tm, d_model), F32), pltpu.SemaphoreType.DMA],
        compiler_params=_cparams("arbitrary"),
        name="moe_combine_residual",
    )(dest, ys, wts, x2d, gate)


def _layer(x2d, mod, norm1_g, w_in, forget_b, pool_w, pool_scale, q_norm_g, k_norm_g, w_out,
           norm2_g, router_w, router_b, w1, b1, w2, b2, *, seq, t):
    n_tok, d_model = x2d.shape
    batch = n_tok // seq
    n_groups, gd, _ = pool_w.shape
    pool_width = n_groups * gd
    att_width = d_model - pool_width
    n_heads = att_width // HEAD_DIM
    n_exp = router_w.shape[1]
    assert pool_width == att_width and n_groups == len(POOL_WINDOWS) and gd % HEAD_DIM == 0
    n_main = pool_width + 3 * att_width
    hp = -(-n_heads // BF16_ROWS) * BF16_ROWS

    shift1, scale1, gate1, shift2, scale2, gate2 = [
        m.reshape(batch, 1, d_model) for m in jnp.split(mod, N_MOD, axis=-1)]

    w_main = w_in[:, :n_main].astype(BF16)
    wf_t = jnp.zeros((hp, d_model), BF16).at[:n_heads].set(w_in[:, n_main:].T.astype(BF16))
    fb = jnp.zeros((hp, 1), F32).at[:n_heads, 0].set(forget_b)
    reps = gd // HEAD_DIM
    main, cum = _in_projection(
        x2d, norm1_g.reshape(1, d_model), scale1, shift1, w_main, wf_t, fb,
        pool_w.astype(BF16), pool_scale.reshape(n_groups, 1, gd),
        jnp.tile(q_norm_g, reps).reshape(1, gd), jnp.tile(k_norm_g, reps).reshape(1, gd),
        seq=seq, t=t)
    att = _attention(main, cum, seq=seq, n_heads=n_heads, q_off=pool_width,
                     k_off=pool_width + att_width, v_off=pool_width + 2 * att_width, t=t)
    x1 = _out_projection(main, att, w_out.astype(BF16), x2d, gate1, seq=seq,
                         pool_width=pool_width, t=t)

    rw = jnp.zeros((d_model, LANES), BF16).at[:, :n_exp].set(router_w.astype(BF16))
    rb = jnp.full((1, LANES), -jnp.inf, F32).at[0, :n_exp].set(router_b)
    packed, idx, wts, rank, counts = _router(
        x1, norm2_g.reshape(1, d_model), scale2, shift2, rw, rb, seq=seq, t=t)

    bm = t["moe_bm"]
    n_assign = n_tok * TOP_K
    n_blocks = -(-n_assign // bm) + n_exp
    counts = counts[0, :n_exp].astype(jnp.int32)
    padded = (counts + bm - 1) // bm * bm
    pad_end = jnp.cumsum(padded)
    pad_start = pad_end - padded
    dest = (pad_start[idx[:, :TOP_K]] + rank[:, :TOP_K]).reshape(-1)
    row_tok = jnp.zeros((n_blocks * bm,), jnp.int32).at[dest].set(
        jnp.arange(n_assign, dtype=jnp.int32) // TOP_K)
    block_e = jnp.minimum(
        jnp.searchsorted(pad_end, jnp.arange(n_blocks, dtype=jnp.int32) * bm, side="right"),
        n_exp - 1).astype(jnp.int32)

    xs = _gather_rows(row_tok, packed, bm=bm)
    ys = _experts(block_e, xs, w1.astype(BF16), b1, w2.astype(BF16), b2, bm=bm)
    return _combine(dest, ys, wts, x1, gate2, seq=seq, t=t)


def kernel(x, c, ada_w, ada_b, norm1_g, w_in, forget_b, pool_w, pool_scale, q_norm_g, k_norm_g,
           w_out, norm2_g, router_w, router_b, expert_w1, expert_b1, expert_w2, expert_b2):
    batch, seq, d_model = x.shape
    depth = ada_w.shape[0]
    t = _tiles(batch * seq, seq, d_model)
    mod = _modulation(c, ada_w, ada_b, t["mod_tn"])
    x2d = x.reshape(batch * seq, d_model)
    for l in range(depth):
        x2d = _layer(x2d, mod[l], norm1_g[l], w_in[l], forget_b[l], pool_w[l], pool_scale[l],
                     q_norm_g[l], k_norm_g[l], w_out[l], norm2_g[l], router_w[l], router_b[l],
                     expert_w1[l], expert_b1[l], expert_w2[l], expert_b2[l], seq=seq, t=t)
    return x2d.reshape(batch, seq, d_model)
```

```python
import functools

import jax
import jax.numpy as jnp
from jax import lax
from jax.experimental import pallas as pl
from jax.experimental.pallas import tpu as pltpu

HEAD_DIM = 128
POOL_WINDOWS = (2, 4, 8, 16)
TOP_K = 4
N_MOD = 6
NORM_EPS = 1e-6
SWIGLU_ALPHA = 1.702
SWIGLU_LIMIT = 7.0
LOG2_E = 1.4426950408889634
ATT_LOGIT_SCALE = HEAD_DIM ** -0.5 * LOG2_E

LANES = 128
SUBLANES = 8
BF16_ROWS = 16
DMA_UNROLL = 8
POOL_HALO = 16
VMEM_LIMIT_BYTES = 56 * 2**20

F32 = jnp.float32
BF16 = jnp.bfloat16


def _cparams(*sem):
    return pltpu.CompilerParams(dimension_semantics=sem, vmem_limit_bytes=VMEM_LIMIT_BYTES)


def _tiles(n_tok, seq, d_model):
    return dict(
        mod_tn=min(512, d_model),
        in_tm=min(512, seq),
        norm_rc=min(64, seq),
        att_tq=min(512, seq),
        att_heads=4,
        out_tm=min(1024, seq),
        out_tn=min(512, d_model),
        rt_tm=min(512, seq),
        moe_bm=min(256, n_tok),
        cmb_tm=min(128, seq),
    )


def _mod_kernel(c_ref, w_ref, b_ref, o_ref):
    c = c_ref[...]
    ca = c / (1.0 + jnp.exp(-c))
    w = w_ref[0].astype(BF16)
    o_ref[0] = jnp.dot(ca.astype(BF16), w, preferred_element_type=F32) + b_ref[0]


def _modulation(c, ada_w, ada_b, tn):
    depth, d_model, n_out = ada_w.shape
    b = c.shape[0]
    c_pad = jnp.zeros((SUBLANES, d_model), F32).at[:b].set(c)
    out = pl.pallas_call(
        _mod_kernel,
        grid=(depth, n_out // tn),
        in_specs=[
            pl.BlockSpec((SUBLANES, d_model), lambda l, j: (0, 0)),
            pl.BlockSpec((1, d_model, tn), lambda l, j: (l, 0, j)),
            pl.BlockSpec((1, 1, tn), lambda l, j: (l, 0, j)),
        ],
        out_specs=pl.BlockSpec((1, SUBLANES, tn), lambda l, j: (l, 0, j)),
        out_shape=jax.ShapeDtypeStruct((depth, SUBLANES, n_out), F32),
        compiler_params=_cparams("arbitrary", "arbitrary"),
        name="adaln_mod",
    )(c_pad, ada_w, ada_b.reshape(depth, 1, n_out))
    return out[:, :b]


def _in_kernel(x_ref, g_ref, sc_ref, sh_ref, w_ref, wf_ref, fb_ref, pw_ref, ps_ref, qg_ref, kg_ref,
               main_ref, cum_ref, h_sc, tail_sc, fc_sc, *, tm, rc, tpb, gd):
    i = pl.program_id(0)
    j = pl.program_id(1)
    first = (i % tpb) == 0

    @pl.when(j == 0)
    def _norm():
        def body(c, carry):
            r0 = pl.multiple_of(c * rc, rc)
            x = x_ref[pl.ds(r0, rc), :]
            ms = jnp.mean(x * x, axis=-1, keepdims=True)
            y = x * lax.rsqrt(ms + NORM_EPS)
            h = (y * g_ref[...]) * (1.0 + sc_ref[0]) + sh_ref[0]
            h_sc[pl.ds(r0, rc), :] = h.astype(BF16)
            return carry
        lax.fori_loop(0, tm // rc, body, 0)

    def proj():
        return jnp.dot(h_sc[...], w_ref[...], preferred_element_type=F32)

    for g, win in enumerate(POOL_WINDOWS):
        @pl.when(j == g)
        def _pool(g=g, win=win):
            u = proj()
            prev = jnp.where(first, 0.0, tail_sc[g])
            s = jnp.concatenate([prev, u], axis=0)
            shift = 1
            while shift < win:
                s = s + pltpu.roll(s, shift, 0)
                shift *= 2
            wsum = s[POOL_HALO:]
            pos = (i % tpb) * tm + lax.broadcasted_iota(jnp.int32, (tm, 1), 0)
            cnt = jnp.minimum(pos + 1, win).astype(F32)
            mixed = wsum / cnt - u
            po = jnp.dot(mixed.astype(BF16), pw_ref[0], preferred_element_type=F32) * ps_ref[0]
            main_ref[...] = po.astype(BF16)
            tail_sc[g] = u[tm - POOL_HALO:]

    def qk_norm(gain_ref, post_scale):
        r = proj()
        for c in range(gd // HEAD_DIM):
            sl = slice(c * HEAD_DIM, (c + 1) * HEAD_DIM)
            rc_ = r[:, sl]
            ms = jnp.mean(rc_ * rc_, axis=-1, keepdims=True)
            normed = (rc_ * lax.rsqrt(ms + NORM_EPS)) * gain_ref[:, sl]
            if post_scale is not None:
                normed = normed * post_scale
            main_ref[:, sl] = normed.astype(BF16)

    @pl.when((j >= 4) & (j < 8))
    def _q():
        qk_norm(qg_ref, ATT_LOGIT_SCALE)

    @pl.when((j >= 8) & (j < 12))
    def _k():
        qk_norm(kg_ref, None)

    @pl.when((j >= 12) & (j < 16))
    def _v():
        main_ref[...] = proj().astype(BF16)

    @pl.when(j == 16)
    def _forget():
        z = lax.dot_general(wf_ref[...], h_sc[...], (((1,), (1,)), ((), ())),
                            preferred_element_type=F32) + fb_ref[...]
        ls = jnp.minimum(z, 0.0) - jnp.log1p(jnp.exp(-jnp.abs(z)))
        p0 = ls.astype(BF16)
        r1 = ls - p0.astype(F32)
        p1 = r1.astype(BF16)
        p2 = (r1 - p1.astype(F32)).astype(BF16)
        rr = lax.broadcasted_iota(jnp.int32, (tm, tm), 0)
        cc = lax.broadcasted_iota(jnp.int32, (tm, tm), 1)
        tri = jnp.where(rr <= cc, 1.0, 0.0).astype(BF16)
        cum = (jnp.dot(p0, tri, preferred_element_type=F32)
               + jnp.dot(p1, tri, preferred_element_type=F32)
               + jnp.dot(p2, tri, preferred_element_type=F32))
        cum = cum + jnp.where(first, 0.0, fc_sc[:, 0:1])
        cum_ref[0] = cum * LOG2_E
        fc_sc[...] = jnp.broadcast_to(cum[:, tm - 1:tm], fc_sc.shape)


def _in_projection(x2d, gain, scale, shift, w_main, wf_t, fb, pool_w, pool_scale, qg, kg, *, seq, t):
    n_tok, d_model = x2d.shape
    batch = n_tok // seq
    tm, rc = t["in_tm"], t["norm_rc"]
    gd = pool_w.shape[-1]
    hp = wf_t.shape[0]
    n_main = w_main.shape[1]
    tpb = seq // tm
    nj = n_main // gd + 1
    last = n_main // gd - 1
    kern = functools.partial(_in_kernel, tm=tm, rc=rc, tpb=tpb, gd=gd)
    return pl.pallas_call(
        kern,
        grid=(n_tok // tm, nj),
        in_specs=[
            pl.BlockSpec((tm, d_model), lambda i, j: (i, 0)),
            pl.BlockSpec((1, d_model), lambda i, j: (0, 0)),
            pl.BlockSpec((1, 1, d_model), lambda i, j: (i // tpb, 0, 0)),
            pl.BlockSpec((1, 1, d_model), lambda i, j: (i // tpb, 0, 0)),
            pl.BlockSpec((d_model, gd), lambda i, j: (0, jnp.minimum(j, last))),
            pl.BlockSpec((hp, d_model), lambda i, j: (0, 0)),
            pl.BlockSpec((hp, 1), lambda i, j: (0, 0)),
            pl.BlockSpec((1, gd, gd), lambda i, j: (jnp.minimum(j, 3), 0, 0)),
            pl.BlockSpec((1, 1, gd), lambda i, j: (jnp.minimum(j, 3), 0, 0)),
            pl.BlockSpec((1, gd), lambda i, j: (0, 0)),
            pl.BlockSpec((1, gd), lambda i, j: (0, 0)),
        ],
        out_specs=[
            pl.BlockSpec((tm, gd), lambda i, j: (i, jnp.minimum(j, last))),
            pl.BlockSpec((1, hp, tm), lambda i, j: (i // tpb, 0, i % tpb)),
        ],
        out_shape=[
            jax.ShapeDtypeStruct((n_tok, n_main), BF16),
            jax.ShapeDtypeStruct((batch, hp, seq), F32),
        ],
        scratch_shapes=[
            pltpu.VMEM((tm, d_model), BF16),
            pltpu.VMEM((len(POOL_WINDOWS), POOL_HALO, gd), F32),
            pltpu.VMEM((hp, LANES), F32),
        ],
        compiler_params=_cparams("arbitrary", "arbitrary"),
        name="norm_in_proj",
    )(x2d, gain, scale, shift, w_main, wf_t, fb, pool_w, pool_scale, qg, kg)


def _attn_kernel(q_ref, k_ref, v_ref, c_ref, o_ref, *, tq, hps):
    qi = pl.program_id(2)
    heads = [slice(h * HEAD_DIM, (h + 1) * HEAD_DIM) for h in range(hps)]

    def block(h, kj, carry, diagonal):
        m_prev, l_prev, acc_prev = carry
        start = pl.multiple_of(kj * tq, tq)
        k = k_ref[pl.ds(start, tq), heads[h]]
        v = v_ref[pl.ds(start, tq), heads[h]]
        s = lax.dot_general(q_ref[:, heads[h]], k, (((1,), (1,)), ((), ())),
                            preferred_element_type=F32)
        s = s - c_ref[h, :, pl.ds(start, tq)]
        if diagonal:
            rr = lax.broadcasted_iota(jnp.int32, (tq, tq), 0)
            cc = lax.broadcasted_iota(jnp.int32, (tq, tq), 1)
            s = jnp.where(cc <= rr, s, -jnp.inf)
        m_new = jnp.maximum(m_prev, jnp.max(s, axis=-1, keepdims=True))
        p = jnp.exp2(s - m_new)
        alpha = jnp.exp2(m_prev - m_new)
        l_new = alpha * l_prev + jnp.sum(p, axis=-1, keepdims=True)
        acc_new = alpha * acc_prev + jnp.dot(p.astype(BF16), v, preferred_element_type=F32)
        return m_new, l_new, acc_new

    def full_blocks(kj, carry):
        return tuple(block(h, kj, carry[h], False) for h in range(hps))

    def diagonal_blocks(_, carry):
        return tuple(block(h, qi, carry[h], True) for h in range(hps))

    init = tuple((jnp.full((tq, 1), -jnp.inf, F32), jnp.zeros((tq, 1), F32),
                  jnp.zeros((tq, HEAD_DIM), F32)) for _ in range(hps))
    carry = lax.fori_loop(0, qi, full_blocks, init)
    carry = lax.fori_loop(0, jnp.minimum(qi + 1, 1), diagonal_blocks, carry)
    for h in range(hps):
        _, l_f, acc_f = carry[h]
        o_ref[:, heads[h]] = (acc_f / l_f).astype(o_ref.dtype)


def _attention(main, cum, *, seq, n_heads, q_off, k_off, v_off, t):
    n_tok = main.shape[0]
    batch = n_tok // seq
    tq, hps = t["att_tq"], min(t["att_heads"], n_heads)
    nq = seq // tq
    hp = cum.shape[1]
    width = hps * HEAD_DIM
    assert n_heads % hps == 0 and hp % hps == 0
    cum3 = cum.reshape(batch * hp, 1, seq)
    kern = functools.partial(_attn_kernel, tq=tq, hps=hps)
    qb, kb, vb, cb = q_off // width, k_off // width, v_off // width, hp // hps
    return pl.pallas_call(
        kern,
        grid=(batch, n_heads // hps, nq),
        in_specs=[
            pl.BlockSpec((tq, width), lambda b, h, i: (b * nq + i, qb + h)),
            pl.BlockSpec((seq, width), lambda b, h, i: (b, kb + h)),
            pl.BlockSpec((seq, width), lambda b, h, i: (b, vb + h)),
            pl.BlockSpec((hps, 1, seq), lambda b, h, i: (b * cb + h, 0, 0)),
        ],
        out_specs=pl.BlockSpec((tq, width), lambda b, h, i: (b * nq + i, h)),
        out_shape=jax.ShapeDtypeStruct((n_tok, n_heads * HEAD_DIM), BF16),
        compiler_params=_cparams("arbitrary", "arbitrary", "arbitrary"),
        name="forget_attention",
    )(main, main, main, cum3)


def _out_kernel(po_ref, at_ref, wp_ref, wa_ref, x_ref, g_ref, o_ref):
    mix = (jnp.dot(po_ref[...], wp_ref[...], preferred_element_type=F32)
           + jnp.dot(at_ref[...], wa_ref[...], preferred_element_type=F32))
    o_ref[...] = x_ref[...] + g_ref[0] * mix


def _out_projection(main, att, w_out, x2d, gate, *, seq, pool_width, t):
    n_tok, d_model = x2d.shape
    tm, tn = t["out_tm"], t["out_tn"]
    att_width = att.shape[1]
    tpb = seq // tm
    return pl.pallas_call(
        _out_kernel,
        grid=(n_tok // tm, d_model // tn),
        in_specs=[
            pl.BlockSpec((tm, pool_width), lambda i, j: (i, 0)),
            pl.BlockSpec((tm, att_width), lambda i, j: (i, 0)),
            pl.BlockSpec((pool_width, tn), lambda i, j: (0, j)),
            pl.BlockSpec((att_width, tn), lambda i, j: (pool_width // att_width, j)),
            pl.BlockSpec((tm, tn), lambda i, j: (i, j)),
            pl.BlockSpec((1, 1, tn), lambda i, j: (i // tpb, 0, j)),
        ],
        out_specs=pl.BlockSpec((tm, tn), lambda i, j: (i, j)),
        out_shape=jax.ShapeDtypeStruct((n_tok, d_model), F32),
        compiler_params=_cparams("arbitrary", "arbitrary"),
        name="out_proj_residual",
    )(main, att, w_out, w_out, x2d, gate)


def _router_kernel(x_ref, g_ref, sc_ref, sh_ref, rw_ref, rb_ref,
                   hp_ref, idx_ref, wts_ref, rank_ref, cnt_ref, h_sc, cnt_sc, *, tm, rc, half):
    i = pl.program_id(0)

    @pl.when(i == 0)
    def _init():
        cnt_sc[...] = jnp.zeros(cnt_sc.shape, F32)

    def body(c, carry):
        r0 = pl.multiple_of(c * rc, rc)
        x = x_ref[pl.ds(r0, rc), :]
        ms = jnp.mean(x * x, axis=-1, keepdims=True)
        y = x * lax.rsqrt(ms + NORM_EPS)
        h = ((y * g_ref[...]) * (1.0 + sc_ref[0]) + sh_ref[0]).astype(BF16)
        h_sc[pl.ds(r0, rc), :] = h
        bits = lax.bitcast_convert_type(h.astype(F32), jnp.uint32)
        lo = lax.shift_right_logical(bits[:, :half], jnp.uint32(16))
        hi = bits[:, half:] & jnp.uint32(0xFFFF0000)
        words = lo | hi
        pitch = half // LANES
        for s in range(pitch):
            hp_ref[pl.ds(r0 * pitch + s, rc, stride=pitch), :] = words[:, s * LANES:(s + 1) * LANES]
        return carry
    lax.fori_loop(0, tm // rc, body, 0)

    logits = jnp.dot(h_sc[...], rw_ref[...], preferred_element_type=F32) + rb_ref[...]
    lane = lax.broadcasted_iota(jnp.int32, logits.shape, 1).astype(F32)
    vals, sels, hots = [], [], []
    cur = logits
    for _ in range(TOP_K):
        mx = jnp.max(cur, axis=-1, keepdims=True)
        sel = jnp.min(jnp.where(cur == mx, lane, float(LANES)), axis=-1, keepdims=True)
        hot = lane == sel
        vals.append(mx)
        sels.append(sel)
        hots.append(hot)
        cur = jnp.where(hot, -jnp.inf, cur)
    exps = [jnp.exp(v - vals[0]) for v in vals]
    denom = exps[0] + exps[1] + exps[2] + exps[3]

    hot_sum = jnp.zeros(logits.shape, F32)
    for hot in hots:
        hot_sum = hot_sum + jnp.where(hot, 1.0, 0.0)
    rr = lax.broadcasted_iota(jnp.int32, (tm, tm), 0)
    cc = lax.broadcasted_iota(jnp.int32, (tm, tm), 1)
    below = jnp.where(cc < rr, 1.0, 0.0).astype(BF16)
    base = jnp.dot(below, hot_sum.astype(BF16), preferred_element_type=F32) + cnt_sc[...]

    idx_out = jnp.zeros(logits.shape, F32)
    wts_out = jnp.zeros(logits.shape, F32)
    rank_out = jnp.zeros(logits.shape, F32)
    for k in range(TOP_K):
        col = lane == float(k)
        rank_k = jnp.sum(jnp.where(hots[k], base, 0.0), axis=-1, keepdims=True)
        idx_out = jnp.where(col, sels[k], idx_out)
        wts_out = jnp.where(col, exps[k] / denom, wts_out)
        rank_out = jnp.where(col, rank_k, rank_out)
    idx_ref[...] = idx_out.astype(jnp.int32)
    wts_ref[...] = wts_out
    rank_ref[...] = rank_out.astype(jnp.int32)
    total = cnt_sc[...] + jnp.sum(hot_sum, axis=0, keepdims=True)
    cnt_sc[...] = total
    cnt_ref[...] = total


def _router(x2d, gain, scale, shift, rw, rb, *, seq, t):
    n_tok, d_model = x2d.shape
    tm, rc = t["rt_tm"], t["norm_rc"]
    tpb = seq // tm
    half = d_model // 2
    kern = functools.partial(_router_kernel, tm=tm, rc=rc, half=half)
    return pl.pallas_call(
        kern,
        grid=(n_tok // tm,),
        in_specs=[
            pl.BlockSpec((tm, d_model), lambda i: (i, 0)),
            pl.BlockSpec((1, d_model), lambda i: (0, 0)),
            pl.BlockSpec((1, 1, d_model), lambda i: (i // tpb, 0, 0)),
            pl.BlockSpec((1, 1, d_model), lambda i: (i // tpb, 0, 0)),
            pl.BlockSpec((d_model, LANES), lambda i: (0, 0)),
            pl.BlockSpec((1, LANES), lambda i: (0, 0)),
        ],
        out_specs=[
            pl.BlockSpec((tm * (half // LANES), LANES), lambda i: (i, 0)),
            pl.BlockSpec((tm, LANES), lambda i: (i, 0)),
            pl.BlockSpec((tm, LANES), lambda i: (i, 0)),
            pl.BlockSpec((tm, LANES), lambda i: (i, 0)),
            pl.BlockSpec((1, LANES), lambda i: (0, 0)),
        ],
        out_shape=[
            jax.ShapeDtypeStruct((n_tok * (half // LANES), LANES), jnp.uint32),
            jax.ShapeDtypeStruct((n_tok, LANES), jnp.int32),
            jax.ShapeDtypeStruct((n_tok, LANES), F32),
            jax.ShapeDtypeStruct((n_tok, LANES), jnp.int32),
            jax.ShapeDtypeStruct((1, LANES), F32),
        ],
        scratch_shapes=[pltpu.VMEM((tm, d_model), BF16), pltpu.VMEM((1, LANES), F32)],
        compiler_params=_cparams("arbitrary"),
        name="norm_router_topk",
    )(x2d, gain, scale, shift, rw, rb)


def _gather_kernel(na_ref, tok_ref, src_ref, o_ref, sem, *, bm, pitch):
    i = pl.program_id(0)

    def slab_copy(r):
        src0 = pl.multiple_of(tok_ref[r] * pitch, pitch)
        dst0 = pl.multiple_of(r * pitch, pitch)
        return pltpu.make_async_copy(src_ref.at[pl.ds(src0, pitch), :],
                                     o_ref.at[pl.ds(dst0, pitch), :], sem)

    @pl.when(i < na_ref[0])
    def _active():
        def issue(g, carry):
            for u in range(DMA_UNROLL):
                slab_copy(g * DMA_UNROLL + u).start(priority=u % 2)
            return carry
        lax.fori_loop(0, bm // DMA_UNROLL, issue, 0)

        def drain(g, carry):
            for u in range(DMA_UNROLL):
                slab_copy(g * DMA_UNROLL + u).wait()
            return carry
        lax.fori_loop(0, bm // DMA_UNROLL, drain, 0)

    @pl.when(i >= na_ref[0])
    def _unused():
        o_ref[...] = jnp.zeros(o_ref.shape, o_ref.dtype)


def _gather_rows(n_active, row_tok, packed, *, bm, pitch):
    n_rows = row_tok.shape[0]
    grid_spec = pltpu.PrefetchScalarGridSpec(
        num_scalar_prefetch=1,
        grid=(n_rows // bm,),
        in_specs=[
            pl.BlockSpec((bm,), lambda i, na: (i,), memory_space=pltpu.SMEM),
            pl.BlockSpec(memory_space=pl.ANY),
        ],
        out_specs=pl.BlockSpec((bm * pitch, LANES), lambda i, na: (i, 0)),
        scratch_shapes=[pltpu.SemaphoreType.DMA],
    )
    return pl.pallas_call(
        functools.partial(_gather_kernel, bm=bm, pitch=pitch),
        grid_spec=grid_spec,
        out_shape=jax.ShapeDtypeStruct((n_rows * pitch, LANES), packed.dtype),
        compiler_params=_cparams("arbitrary"),
        name="moe_row_gather",
    )(n_active, row_tok, packed)


def _expert_kernel(be_ref, na_ref, x_ref, w1_ref, b1_ref, w2_ref, b2_ref, o_ref,
                   *, bm, half, fdim, xp, yp):
    del be_ref
    i = pl.program_id(0)
    chunks = 2 * half // LANES

    @pl.when(i < na_ref[0])
    def _active():
        words = jnp.concatenate([x_ref[pl.ds(s, bm, stride=xp), :] for s in range(xp)], axis=1)
        lo = lax.bitcast_convert_type(lax.shift_left(words, jnp.uint32(16)), F32).astype(BF16)
        hi = lax.bitcast_convert_type(words & jnp.uint32(0xFFFF0000), F32).astype(BF16)
        hid = (jnp.dot(lo, w1_ref[0, :half, :], preferred_element_type=F32)
               + jnp.dot(hi, w1_ref[0, half:, :], preferred_element_type=F32)) + b1_ref[0]
        glu = jnp.minimum(hid[:, :fdim], SWIGLU_LIMIT)
        lin = jnp.clip(hid[:, fdim:], -SWIGLU_LIMIT, SWIGLU_LIMIT)
        act = (glu / (1.0 + jnp.exp(-SWIGLU_ALPHA * glu))) * (lin + 1.0)
        y = jnp.dot(act.astype(BF16), w2_ref[0], preferred_element_type=F32) + b2_ref[0]
        for c in range(chunks):
            o_ref[pl.ds(c, bm, stride=yp), :] = y[:, c * LANES:(c + 1) * LANES]
        for c in range(chunks, yp):
            o_ref[pl.ds(c, bm, stride=yp), :] = jnp.zeros((bm, LANES), F32)

    @pl.when(i >= na_ref[0])
    def _unused():
        o_ref[...] = jnp.zeros(o_ref.shape, o_ref.dtype)


def _slab_pitch(chunks):
    groups = -(-chunks // SUBLANES)
    return (groups + 1 - groups % 2) * SUBLANES


def _experts(block_e, n_active, xs, w1, b1, w2, b2, *, bm, xp):
    n_rows = xs.shape[0] // xp
    n_exp, d_model, two_f = w1.shape
    fdim, half = two_f // 2, d_model // 2
    yp = _slab_pitch(d_model // LANES)
    grid_spec = pltpu.PrefetchScalarGridSpec(
        num_scalar_prefetch=2,
        grid=(n_rows // bm,),
        in_specs=[
            pl.BlockSpec((bm * xp, LANES), lambda i, be, na: (i, 0)),
            pl.BlockSpec((1, d_model, two_f), lambda i, be, na: (be[i], 0, 0)),
            pl.BlockSpec((1, 1, two_f), lambda i, be, na: (be[i], 0, 0)),
            pl.BlockSpec((1, fdim, d_model), lambda i, be, na: (be[i], 0, 0)),
            pl.BlockSpec((1, 1, d_model), lambda i, be, na: (be[i], 0, 0)),
        ],
        out_specs=pl.BlockSpec((bm * yp, LANES), lambda i, be, na: (i, 0)),
    )
    ys = pl.pallas_call(
        functools.partial(_expert_kernel, bm=bm, half=half, fdim=fdim, xp=xp, yp=yp),
        grid_spec=grid_spec,
        out_shape=jax.ShapeDtypeStruct((n_rows * yp, LANES), F32),
        compiler_params=_cparams("arbitrary"),
        name="moe_experts",
    )(block_e, n_active, xs, w1, b1.reshape(n_exp, 1, two_f), w2, b2.reshape(n_exp, 1, d_model))
    return ys, yp


def _combine_kernel(dest_ref, y_ref, wts_ref, x_ref, g_ref, o_ref, buf, sem, *, tm, yp, chunks):
    def slab_copy(r, k):
        src0 = pl.multiple_of(dest_ref[r * TOP_K + k] * yp, SUBLANES)
        dst0 = pl.multiple_of(r * yp, SUBLANES)
        return pltpu.make_async_copy(y_ref.at[pl.ds(src0, chunks), :],
                                     buf.at[k, pl.ds(dst0, chunks), :], sem)

    def issue(g, carry):
        for u in range(DMA_UNROLL // TOP_K):
            for k in range(TOP_K):
                slab_copy(g * (DMA_UNROLL // TOP_K) + u, k).start(priority=k % 2)
        return carry
    lax.fori_loop(0, tm * TOP_K // DMA_UNROLL, issue, 0)

    def drain(g, carry):
        for u in range(DMA_UNROLL // TOP_K):
            for k in range(TOP_K):
                slab_copy(g * (DMA_UNROLL // TOP_K) + u, k).wait()
        return carry
    lax.fori_loop(0, tm * TOP_K // DMA_UNROLL, drain, 0)

    w = wts_ref[...]
    wk = [jnp.broadcast_to(w[:, k:k + 1], (tm, LANES)) for k in range(TOP_K)]
    for c in range(chunks):
        acc = wk[0] * buf[0, pl.ds(c, tm, stride=yp), :]
        for k in range(1, TOP_K):
            acc = acc + wk[k] * buf[k, pl.ds(c, tm, stride=yp), :]
        sl = slice(c * LANES, (c + 1) * LANES)
        o_ref[:, sl] = x_ref[:, sl] + g_ref[0, :, sl] * acc


def _combine(dest, ys, yp, wts, x2d, gate, *, seq, t):
    n_tok, d_model = x2d.shape
    tm = t["cmb_tm"]
    tpb = seq // tm
    chunks = d_model // LANES
    return pl.pallas_call(
        functools.partial(_combine_kernel, tm=tm, yp=yp, chunks=chunks),
        grid=(n_tok // tm,),
        in_specs=[
            pl.BlockSpec((tm * TOP_K,), lambda i: (i,), memory_space=pltpu.SMEM),
            pl.BlockSpec(memory_space=pl.ANY),
            pl.BlockSpec((tm, LANES), lambda i: (i, 0)),
            pl.BlockSpec((tm, d_model), lambda i: (i, 0)),
            pl.BlockSpec((1, 1, d_model), lambda i: (i // tpb, 0, 0)),
        ],
        out_specs=pl.BlockSpec((tm, d_model), lambda i: (i, 0)),
        out_shape=jax.ShapeDtypeStruct((n_tok, d_model), F32),
        scratch_shapes=[pltpu.VMEM((TOP_K, tm * yp, LANES), F32), pltpu.SemaphoreType.DMA],
        compiler_params=_cparams("arbitrary"),
        name="moe_combine_residual",
    )(dest, ys, wts, x2d, gate)


def _layer(x2d, mod, norm1_g, w_in, forget_b, pool_w, pool_scale, q_norm_g, k_norm_g, w_out,
           norm2_g, router_w, router_b, w1, b1, w2, b2, *, seq, t):
    n_tok, d_model = x2d.shape
    batch = n_tok // seq
    n_groups, gd, _ = pool_w.shape
    pool_width = n_groups * gd
    att_width = d_model - pool_width
    n_heads = att_width // HEAD_DIM
    n_exp = router_w.shape[1]
    assert pool_width == att_width and n_groups == len(POOL_WINDOWS) and gd % HEAD_DIM == 0
    n_main = pool_width + 3 * att_width
    hp = -(-n_heads // BF16_ROWS) * BF16_ROWS

    shift1, scale1, gate1, shift2, scale2, gate2 = [
        m.reshape(batch, 1, d_model) for m in jnp.split(mod, N_MOD, axis=-1)]

    w_main = w_in[:, :n_main].astype(BF16)
    wf_t = jnp.zeros((hp, d_model), BF16).at[:n_heads].set(w_in[:, n_main:].T.astype(BF16))
    fb = jnp.zeros((hp, 1), F32).at[:n_heads, 0].set(forget_b)
    reps = gd // HEAD_DIM
    main, cum = _in_projection(
        x2d, norm1_g.reshape(1, d_model), scale1, shift1, w_main, wf_t, fb,
        pool_w.astype(BF16), pool_scale.reshape(n_groups, 1, gd),
        jnp.tile(q_norm_g, reps).reshape(1, gd), jnp.tile(k_norm_g, reps).reshape(1, gd),
        seq=seq, t=t)
    att = _attention(main, cum, seq=seq, n_heads=n_heads, q_off=pool_width,
                     k_off=pool_width + att_width, v_off=pool_width + 2 * att_width, t=t)
    x1 = _out_projection(main, att, w_out.astype(BF16), x2d, gate1, seq=seq,
                         pool_width=pool_width, t=t)

    rw = jnp.zeros((d_model, LANES), BF16).at[:, :n_exp].set(router_w.astype(BF16))
    rb = jnp.full((1, LANES), -jnp.inf, F32).at[0, :n_exp].set(router_b)
    packed, idx, wts, rank, counts = _router(
        x1, norm2_g.reshape(1, d_model), scale2, shift2, rw, rb, seq=seq, t=t)

    bm = t["moe_bm"]
    n_assign = n_tok * TOP_K
    n_blocks = -(-n_assign // bm) + n_exp
    counts = counts[0, :n_exp].astype(jnp.int32)
    padded = (counts + bm - 1) // bm * bm
    pad_end = jnp.cumsum(padded)
    pad_start = pad_end - padded
    dest = (pad_start[idx[:, :TOP_K]] + rank[:, :TOP_K]).reshape(-1)
    row_tok = jnp.zeros((n_blocks * bm,), jnp.int32).at[dest].set(
        jnp.arange(n_assign, dtype=jnp.int32) // TOP_K,
        unique_indices=True, mode="promise_in_bounds")
    block_start = jnp.arange(n_blocks, dtype=jnp.int32) * bm
    block_e = jnp.minimum(jnp.sum(block_start[:, None] >= pad_end[None, :], axis=1),
                          n_exp - 1).astype(jnp.int32)
    n_active = (pad_end[-1:] // bm).astype(jnp.int32)

    xp = d_model // 2 // LANES
    xs = _gather_rows(n_active, row_tok, packed, bm=bm, pitch=xp)
    ys, yp = _experts(block_e, n_active, xs, w1.astype(BF16), b1, w2.astype(BF16), b2, bm=bm, xp=xp)
    return _combine(dest, ys, yp, wts, x1, gate2, seq=seq, t=t)


def kernel(x, c, ada_w, ada_b, norm1_g, w_in, forget_b, pool_w, pool_scale, q_norm_g, k_norm_g,
           w_out, norm2_g, router_w, router_b, expert_w1, expert_b1, expert_w2, expert_b2):
    batch, seq, d_model = x.shape
    depth = ada_w.shape[0]
    t = _tiles(batch * seq, seq, d_model)
    mod = _modulation(c, ada_w, ada_b, t["mod_tn"])
    x2d = x.reshape(batch * seq, d_model)
    for l in range(depth):
        x2d = _layer(x2d, mod[l], norm1_g[l], w_in[l], forget_b[l], pool_w[l], pool_scale[l],
                     q_norm_g[l], k_norm_g[l], w_out[l], norm2_g[l], router_w[l], router_b[l],
                     expert_w1[l], expert_b1[l], expert_w2[l], expert_b2[l], seq=seq, t=t)
    return x2d.reshape(batch, seq, d_model)
```

```python
import functools

import jax
import jax.numpy as jnp
from jax import lax
from jax.experimental import pallas as pl
from jax.experimental.pallas import tpu as pltpu

HEAD_DIM = 128
POOL_WINDOWS = (2, 4, 8, 16)
TOP_K = 4
N_MOD = 6
NORM_EPS = 1e-6
SWIGLU_ALPHA = 1.702
SWIGLU_LIMIT = 7.0
LOG2_E = 1.4426950408889634
ATT_LOGIT_SCALE = HEAD_DIM ** -0.5 * LOG2_E

LANES = 128
SUBLANES = 8
BF16_ROWS = 16
DMA_UNROLL = 8
POOL_HALO = 16
VMEM_LIMIT_BYTES = 56 * 2**20

F32 = jnp.float32
BF16 = jnp.bfloat16


def _cparams(*sem):
    return pltpu.CompilerParams(dimension_semantics=sem, vmem_limit_bytes=VMEM_LIMIT_BYTES)


def _tiles(n_tok, seq, d_model):
    return dict(
        mod_tn=min(512, d_model),
        in_tm=min(512, seq),
        norm_rc=min(64, seq),
        att_tq=min(512, seq),
        att_heads=4,
        out_tm=min(1024, seq),
        out_tn=min(512, d_model),
        rt_tm=min(512, seq),
        moe_bm=min(256, n_tok),
        cmb_tm=min(128, seq),
    )


def _mod_kernel(c_ref, w_ref, b_ref, o_ref):
    c = c_ref[...]
    ca = c / (1.0 + jnp.exp(-c))
    w = w_ref[0].astype(BF16)
    o_ref[0] = jnp.dot(ca.astype(BF16), w, preferred_element_type=F32) + b_ref[0]


def _modulation(c, ada_w, ada_b, tn):
    depth, d_model, n_out = ada_w.shape
    b = c.shape[0]
    c_pad = jnp.zeros((SUBLANES, d_model), F32).at[:b].set(c)
    out = pl.pallas_call(
        _mod_kernel,
        grid=(depth, n_out // tn),
        in_specs=[
            pl.BlockSpec((SUBLANES, d_model), lambda l, j: (0, 0)),
            pl.BlockSpec((1, d_model, tn), lambda l, j: (l, 0, j)),
            pl.BlockSpec((1, 1, tn), lambda l, j: (l, 0, j)),
        ],
        out_specs=pl.BlockSpec((1, SUBLANES, tn), lambda l, j: (l, 0, j)),
        out_shape=jax.ShapeDtypeStruct((depth, SUBLANES, n_out), F32),
        compiler_params=_cparams("arbitrary", "arbitrary"),
        name="adaln_mod",
    )(c_pad, ada_w, ada_b.reshape(depth, 1, n_out))
    return out[:, :b]


def _in_kernel(x_ref, g_ref, sc_ref, sh_ref, w_ref, wf_ref, fb_ref, pw_ref, ps_ref, qg_ref, kg_ref,
               main_ref, cum_ref, h_sc, tail_sc, fc_sc, *, tm, rc, tpb, gd):
    i = pl.program_id(0)
    j = pl.program_id(1)
    first = (i % tpb) == 0

    @pl.when(j == 0)
    def _norm():
        def body(c, carry):
            r0 = pl.multiple_of(c * rc, rc)
            x = x_ref[pl.ds(r0, rc), :]
            ms = jnp.mean(x * x, axis=-1, keepdims=True)
            y = x * lax.rsqrt(ms + NORM_EPS)
            h = (y * g_ref[...]) * (1.0 + sc_ref[0]) + sh_ref[0]
            h_sc[pl.ds(r0, rc), :] = h.astype(BF16)
            return carry
        lax.fori_loop(0, tm // rc, body, 0)

    def proj():
        return jnp.dot(h_sc[...], w_ref[...], preferred_element_type=F32)

    for g, win in enumerate(POOL_WINDOWS):
        @pl.when(j == g)
        def _pool(g=g, win=win):
            u = proj()
            prev = jnp.where(first, 0.0, tail_sc[g])
            s = jnp.concatenate([prev, u], axis=0)
            shift = 1
            while shift < win:
                s = s + pltpu.roll(s, shift, 0)
                shift *= 2
            wsum = s[POOL_HALO:]
            pos = (i % tpb) * tm + lax.broadcasted_iota(jnp.int32, (tm, 1), 0)
            cnt = jnp.minimum(pos + 1, win).astype(F32)
            mixed = wsum / cnt - u
            po = jnp.dot(mixed.astype(BF16), pw_ref[0], preferred_element_type=F32) * ps_ref[0]
            main_ref[...] = po.astype(BF16)
            tail_sc[g] = u[tm - POOL_HALO:]

    def qk_norm(gain_ref, post_scale):
        r = proj()
        for c in range(gd // HEAD_DIM):
            sl = slice(c * HEAD_DIM, (c + 1) * HEAD_DIM)
            rc_ = r[:, sl]
            ms = jnp.mean(rc_ * rc_, axis=-1, keepdims=True)
            normed = (rc_ * lax.rsqrt(ms + NORM_EPS)) * gain_ref[:, sl]
            if post_scale is not None:
                normed = normed * post_scale
            main_ref[:, sl] = normed.astype(BF16)

    @pl.when((j >= 4) & (j < 8))
    def _q():
        qk_norm(qg_ref, ATT_LOGIT_SCALE)

    @pl.when((j >= 8) & (j < 12))
    def _k():
        qk_norm(kg_ref, None)

    @pl.when((j >= 12) & (j < 16))
    def _v():
        main_ref[...] = proj().astype(BF16)

    @pl.when(j == 16)
    def _forget():
        z = lax.dot_general(wf_ref[...], h_sc[...], (((1,), (1,)), ((), ())),
                            preferred_element_type=F32) + fb_ref[...]
        ls = jnp.minimum(z, 0.0) - jnp.log1p(jnp.exp(-jnp.abs(z)))
        p0 = ls.astype(BF16)
        r1 = ls - p0.astype(F32)
        p1 = r1.astype(BF16)
        p2 = (r1 - p1.astype(F32)).astype(BF16)
        rr = lax.broadcasted_iota(jnp.int32, (tm, tm), 0)
        cc = lax.broadcasted_iota(jnp.int32, (tm, tm), 1)
        tri = jnp.where(rr <= cc, 1.0, 0.0).astype(BF16)
        cum = (jnp.dot(p0, tri, preferred_element_type=F32)
               + jnp.dot(p1, tri, preferred_element_type=F32)
               + jnp.dot(p2, tri, preferred_element_type=F32))
        cum = cum + jnp.where(first, 0.0, fc_sc[:, 0:1])
        cum_ref[0] = cum * LOG2_E
        fc_sc[...] = jnp.broadcast_to(cum[:, tm - 1:tm], fc_sc.shape)


def _in_projection(x2d, gain, scale, shift, w_main, wf_t, fb, pool_w, pool_scale, qg, kg, *, seq, t):
    n_tok, d_model = x2d.shape
    batch = n_tok // seq
    tm, rc = t["in_tm"], t["norm_rc"]
    gd = pool_w.shape[-1]
    hp = wf_t.shape[0]
    n_main = w_main.shape[1]
    tpb = seq // tm
    nj = n_main // gd + 1
    last = n_main // gd - 1
    kern = functools.partial(_in_kernel, tm=tm, rc=rc, tpb=tpb, gd=gd)
    return pl.pallas_call(
        kern,
        grid=(n_tok // tm, nj),
        in_specs=[
            pl.BlockSpec((tm, d_model), lambda i, j: (i, 0)),
            pl.BlockSpec((1, d_model), lambda i, j: (0, 0)),
            pl.BlockSpec((1, 1, d_model), lambda i, j: (i // tpb, 0, 0)),
            pl.BlockSpec((1, 1, d_model), lambda i, j: (i // tpb, 0, 0)),
            pl.BlockSpec((d_model, gd), lambda i, j: (0, jnp.minimum(j, last))),
            pl.BlockSpec((hp, d_model), lambda i, j: (0, 0)),
            pl.BlockSpec((hp, 1), lambda i, j: (0, 0)),
            pl.BlockSpec((1, gd, gd), lambda i, j: (jnp.minimum(j, 3), 0, 0)),
            pl.BlockSpec((1, 1, gd), lambda i, j: (jnp.minimum(j, 3), 0, 0)),
            pl.BlockSpec((1, gd), lambda i, j: (0, 0)),
            pl.BlockSpec((1, gd), lambda i, j: (0, 0)),
        ],
        out_specs=[
            pl.BlockSpec((tm, gd), lambda i, j: (i, jnp.minimum(j, last))),
            pl.BlockSpec((1, hp, tm), lambda i, j: (i // tpb, 0, i % tpb)),
        ],
        out_shape=[
            jax.ShapeDtypeStruct((n_tok, n_main), BF16),
            jax.ShapeDtypeStruct((batch, hp, seq), F32),
        ],
        scratch_shapes=[
            pltpu.VMEM((tm, d_model), BF16),
            pltpu.VMEM((len(POOL_WINDOWS), POOL_HALO, gd), F32),
            pltpu.VMEM((hp, LANES), F32),
        ],
        compiler_params=_cparams("arbitrary", "arbitrary"),
        name="norm_in_proj",
    )(x2d, gain, scale, shift, w_main, wf_t, fb, pool_w, pool_scale, qg, kg)


def _attn_kernel(q_ref, k_ref, v_ref, c_ref, o_ref, *, tq, hps):
    qi = pl.program_id(2)
    heads = [slice(h * HEAD_DIM, (h + 1) * HEAD_DIM) for h in range(hps)]

    def block(h, kj, carry, diagonal):
        m_prev, l_prev, acc_prev = carry
        start = pl.multiple_of(kj * tq, tq)
        k = k_ref[pl.ds(start, tq), heads[h]]
        v = v_ref[pl.ds(start, tq), heads[h]]
        s = lax.dot_general(q_ref[:, heads[h]], k, (((1,), (1,)), ((), ())),
                            preferred_element_type=F32)
        s = s - c_ref[h, :, pl.ds(start, tq)]
        if diagonal:
            rr = lax.broadcasted_iota(jnp.int32, (tq, tq), 0)
            cc = lax.broadcasted_iota(jnp.int32, (tq, tq), 1)
            s = jnp.where(cc <= rr, s, -jnp.inf)
        m_new = jnp.maximum(m_prev, jnp.max(s, axis=-1, keepdims=True))
        p = jnp.exp2(s - m_new)
        alpha = jnp.exp2(m_prev - m_new)
        l_new = alpha * l_prev + jnp.sum(p, axis=-1, keepdims=True)
        acc_new = alpha * acc_prev + jnp.dot(p.astype(BF16), v, preferred_element_type=F32)
        return m_new, l_new, acc_new

    def full_blocks(kj, carry):
        return tuple(block(h, kj, carry[h], False) for h in range(hps))

    def diagonal_blocks(_, carry):
        return tuple(block(h, qi, carry[h], True) for h in range(hps))

    init = tuple((jnp.full((tq, 1), -jnp.inf, F32), jnp.zeros((tq, 1), F32),
                  jnp.zeros((tq, HEAD_DIM), F32)) for _ in range(hps))
    carry = lax.fori_loop(0, qi, full_blocks, init)
    carry = lax.fori_loop(0, jnp.minimum(qi + 1, 1), diagonal_blocks, carry)
    for h in range(hps):
        _, l_f, acc_f = carry[h]
        o_ref[:, heads[h]] = (acc_f / l_f).astype(o_ref.dtype)


def _attention(main, cum, *, seq, n_heads, q_off, k_off, v_off, t):
    n_tok = main.shape[0]
    batch = n_tok // seq
    tq, hps = t["att_tq"], min(t["att_heads"], n_heads)
    nq = seq // tq
    hp = cum.shape[1]
    width = hps * HEAD_DIM
    assert n_heads % hps == 0 and hp % hps == 0
    cum3 = cum.reshape(batch * hp, 1, seq)
    kern = functools.partial(_attn_kernel, tq=tq, hps=hps)
    qb, kb, vb, cb = q_off // width, k_off // width, v_off // width, hp // hps
    return pl.pallas_call(
        kern,
        grid=(batch, n_heads // hps, nq),
        in_specs=[
            pl.BlockSpec((tq, width), lambda b, h, i: (b * nq + i, qb + h)),
            pl.BlockSpec((seq, width), lambda b, h, i: (b, kb + h)),
            pl.BlockSpec((seq, width), lambda b, h, i: (b, vb + h)),
            pl.BlockSpec((hps, 1, seq), lambda b, h, i: (b * cb + h, 0, 0)),
        ],
        out_specs=pl.BlockSpec((tq, width), lambda b, h, i: (b * nq + i, h)),
        out_shape=jax.ShapeDtypeStruct((n_tok, n_heads * HEAD_DIM), BF16),
        compiler_params=_cparams("arbitrary", "arbitrary", "arbitrary"),
        name="forget_attention",
    )(main, main, main, cum3)


def _out_kernel(po_ref, at_ref, wp_ref, wa_ref, x_ref, g_ref, o_ref):
    mix = (jnp.dot(po_ref[...], wp_ref[...], preferred_element_type=F32)
           + jnp.dot(at_ref[...], wa_ref[...], preferred_element_type=F32))
    o_ref[...] = x_ref[...] + g_ref[0] * mix


def _out_projection(main, att, w_out, x2d, gate, *, seq, pool_width, t):
    n_tok, d_model = x2d.shape
    tm, tn = t["out_tm"], t["out_tn"]
    att_width = att.shape[1]
    tpb = seq // tm
    return pl.pallas_call(
        _out_kernel,
        grid=(n_tok // tm, d_model // tn),
        in_specs=[
            pl.BlockSpec((tm, pool_width), lambda i, j: (i, 0)),
            pl.BlockSpec((tm, att_width), lambda i, j: (i, 0)),
            pl.BlockSpec((pool_width, tn), lambda i, j: (0, j)),
            pl.BlockSpec((att_width, tn), lambda i, j: (pool_width // att_width, j)),
            pl.BlockSpec((tm, tn), lambda i, j: (i, j)),
            pl.BlockSpec((1, 1, tn), lambda i, j: (i // tpb, 0, j)),
        ],
        out_specs=pl.BlockSpec((tm, tn), lambda i, j: (i, j)),
        out_shape=jax.ShapeDtypeStruct((n_tok, d_model), F32),
        compiler_params=_cparams("arbitrary", "arbitrary"),
        name="out_proj_residual",
    )(main, att, w_out, w_out, x2d, gate)


def _router_kernel(x_ref, g_ref, sc_ref, sh_ref, rw_ref, rb_ref,
                   hp_ref, idx_ref, wts_ref, rank_ref, cnt_ref, h_sc, cnt_sc, *, tm, rc, half):
    i = pl.program_id(0)

    @pl.when(i == 0)
    def _init():
        cnt_sc[...] = jnp.zeros(cnt_sc.shape, F32)

    def body(c, carry):
        r0 = pl.multiple_of(c * rc, rc)
        x = x_ref[pl.ds(r0, rc), :]
        ms = jnp.mean(x * x, axis=-1, keepdims=True)
        y = x * lax.rsqrt(ms + NORM_EPS)
        h = ((y * g_ref[...]) * (1.0 + sc_ref[0]) + sh_ref[0]).astype(BF16)
        h_sc[pl.ds(r0, rc), :] = h
        bits = lax.bitcast_convert_type(h.astype(F32), jnp.uint32)
        lo = lax.shift_right_logical(bits[:, :half], jnp.uint32(16))
        hi = bits[:, half:] & jnp.uint32(0xFFFF0000)
        words = lo | hi
        pitch = half // LANES
        for s in range(pitch):
            hp_ref[pl.ds(r0 * pitch + s, rc, stride=pitch), :] = words[:, s * LANES:(s + 1) * LANES]
        return carry
    lax.fori_loop(0, tm // rc, body, 0)

    logits = jnp.dot(h_sc[...], rw_ref[...], preferred_element_type=F32) + rb_ref[...]
    lane = lax.broadcasted_iota(jnp.int32, logits.shape, 1).astype(F32)
    vals, sels, hots = [], [], []
    cur = logits
    for _ in range(TOP_K):
        mx = jnp.max(cur, axis=-1, keepdims=True)
        sel = jnp.min(jnp.where(cur == mx, lane, float(LANES)), axis=-1, keepdims=True)
        hot = lane == sel
        vals.append(mx)
        sels.append(sel)
        hots.append(hot)
        cur = jnp.where(hot, -jnp.inf, cur)
    exps = [jnp.exp(v - vals[0]) for v in vals]
    denom = exps[0] + exps[1] + exps[2] + exps[3]

    hot_sum = jnp.zeros(logits.shape, F32)
    for hot in hots:
        hot_sum = hot_sum + jnp.where(hot, 1.0, 0.0)
    rr = lax.broadcasted_iota(jnp.int32, (tm, tm), 0)
    cc = lax.broadcasted_iota(jnp.int32, (tm, tm), 1)
    below = jnp.where(cc < rr, 1.0, 0.0).astype(BF16)
    base = jnp.dot(below, hot_sum.astype(BF16), preferred_element_type=F32) + cnt_sc[...]

    idx_out = jnp.zeros(logits.shape, F32)
    wts_out = jnp.zeros(logits.shape, F32)
    rank_out = jnp.zeros(logits.shape, F32)
    for k in range(TOP_K):
        col = lane == float(k)
        rank_k = jnp.sum(jnp.where(hots[k], base, 0.0), axis=-1, keepdims=True)
        idx_out = jnp.where(col, sels[k], idx_out)
        wts_out = jnp.where(col, exps[k] / denom, wts_out)
        rank_out = jnp.where(col, rank_k, rank_out)
    idx_ref[...] = idx_out.astype(jnp.int32)
    wts_ref[...] = wts_out
    rank_ref[...] = rank_out.astype(jnp.int32)
    total = cnt_sc[...] + jnp.sum(hot_sum, axis=0, keepdims=True)
    cnt_sc[...] = total
    cnt_ref[...] = total


def _router(x2d, gain, scale, shift, rw, rb, *, seq, t):
    n_tok, d_model = x2d.shape
    tm, rc = t["rt_tm"], t["norm_rc"]
    tpb = seq // tm
    half = d_model // 2
    kern = functools.partial(_router_kernel, tm=tm, rc=rc, half=half)
    return pl.pallas_call(
        kern,
        grid=(n_tok // tm,),
        in_specs=[
            pl.BlockSpec((tm, d_model), lambda i: (i, 0)),
            pl.BlockSpec((1, d_model), lambda i: (0, 0)),
            pl.BlockSpec((1, 1, d_model), lambda i: (i // tpb, 0, 0)),
            pl.BlockSpec((1, 1, d_model), lambda i: (i // tpb, 0, 0)),
            pl.BlockSpec((d_model, LANES), lambda i: (0, 0)),
            pl.BlockSpec((1, LANES), lambda i: (0, 0)),
        ],
        out_specs=[
            pl.BlockSpec((tm * (half // LANES), LANES), lambda i: (i, 0)),
            pl.BlockSpec((tm, LANES), lambda i: (i, 0)),
            pl.BlockSpec((tm, LANES), lambda i: (i, 0)),
            pl.BlockSpec((tm, LANES), lambda i: (i, 0)),
            pl.BlockSpec((1, LANES), lambda i: (0, 0)),
        ],
        out_shape=[
            jax.ShapeDtypeStruct((n_tok * (half // LANES), LANES), jnp.uint32),
            jax.ShapeDtypeStruct((n_tok, LANES), jnp.int32),
            jax.ShapeDtypeStruct((n_tok, LANES), F32),
            jax.ShapeDtypeStruct((n_tok, LANES), jnp.int32),
            jax.ShapeDtypeStruct((1, LANES), F32),
        ],
        scratch_shapes=[pltpu.VMEM((tm, d_model), BF16), pltpu.VMEM((1, LANES), F32)],
        compiler_params=_cparams("arbitrary"),
        name="norm_router_topk",
    )(x2d, gain, scale, shift, rw, rb)


def _expert_kernel(be_ref, na_ref, tok_ref, nxt_ref, src_ref, w1_ref, b1_ref, w2_ref, b2_ref,
                   o_ref, xbuf, sems, *, bm, half, fdim, xp, yp):
    del be_ref
    i = pl.program_id(0)
    n_active = na_ref[0]
    chunks = 2 * half // LANES

    def slab_copy(toks, r, slot):
        src0 = pl.multiple_of(toks[r] * xp, xp)
        dst0 = pl.multiple_of(r * xp, xp)
        return pltpu.make_async_copy(src_ref.at[pl.ds(src0, xp), :],
                                     xbuf.at[slot, pl.ds(dst0, xp), :], sems.at[slot])

    def gather_start(toks, slot):
        def issue(g, carry):
            for u in range(DMA_UNROLL):
                slab_copy(toks, g * DMA_UNROLL + u, slot).start(priority=u % 2)
            return carry
        lax.fori_loop(0, bm // DMA_UNROLL, issue, 0)

    def gather_wait(toks, slot):
        def drain(g, carry):
            for u in range(DMA_UNROLL):
                slab_copy(toks, g * DMA_UNROLL + u, slot).wait()
            return carry
        lax.fori_loop(0, bm // DMA_UNROLL, drain, 0)

    def compute(slot):
        words = jnp.concatenate([xbuf[slot, pl.ds(s, bm, stride=xp), :] for s in range(xp)], axis=1)
        lo = lax.bitcast_convert_type(lax.shift_left(words, jnp.uint32(16)), F32).astype(BF16)
        hi = lax.bitcast_convert_type(words & jnp.uint32(0xFFFF0000), F32).astype(BF16)
        hid = (jnp.dot(lo, w1_ref[0, :half, :], preferred_element_type=F32)
               + jnp.dot(hi, w1_ref[0, half:, :], preferred_element_type=F32)) + b1_ref[0]
        glu = jnp.minimum(hid[:, :fdim], SWIGLU_LIMIT)
        lin = jnp.clip(hid[:, fdim:], -SWIGLU_LIMIT, SWIGLU_LIMIT)
        act = (glu / (1.0 + jnp.exp(-SWIGLU_ALPHA * glu))) * (lin + 1.0)
        y = jnp.dot(act.astype(BF16), w2_ref[0], preferred_element_type=F32) + b2_ref[0]
        for c in range(chunks):
            o_ref[pl.ds(c, bm, stride=yp), :] = y[:, c * LANES:(c + 1) * LANES]
        for c in range(chunks, yp):
            o_ref[pl.ds(c, bm, stride=yp), :] = jnp.zeros((bm, LANES), F32)

    @pl.when(i == 0)
    def _first():
        gather_start(tok_ref, 0)

    for slot in range(2):
        @pl.when((i % 2 == slot) & (i + 1 < n_active))
        def _prefetch(slot=slot):
            gather_start(nxt_ref, 1 - slot)

        @pl.when((i % 2 == slot) & (i < n_active))
        def _active(slot=slot):
            gather_wait(tok_ref, slot)
            compute(slot)

    @pl.when(i >= n_active)
    def _unused():
        o_ref[...] = jnp.zeros(o_ref.shape, o_ref.dtype)


def _slab_pitch(chunks):
    groups = -(-chunks // SUBLANES)
    return (groups + 1 - groups % 2) * SUBLANES


def _experts(block_e, n_active, row_tok, packed, w1, b1, w2, b2, *, bm, xp):
    n_blocks = row_tok.shape[0] // bm
    n_exp, d_model, two_f = w1.shape
    fdim, half = two_f // 2, d_model // 2
    yp = _slab_pitch(d_model // LANES)
    grid_spec = pltpu.PrefetchScalarGridSpec(
        num_scalar_prefetch=2,
        grid=(n_blocks,),
        in_specs=[
            pl.BlockSpec((bm,), lambda i, be, na: (i,), memory_space=pltpu.SMEM),
            pl.BlockSpec((bm,), lambda i, be, na: (jnp.minimum(i + 1, n_blocks - 1),),
                         memory_space=pltpu.SMEM),
            pl.BlockSpec(memory_space=pl.ANY),
            pl.BlockSpec((1, d_model, two_f), lambda i, be, na: (be[i], 0, 0)),
            pl.BlockSpec((1, 1, two_f), lambda i, be, na: (be[i], 0, 0)),
            pl.BlockSpec((1, fdim, d_model), lambda i, be, na: (be[i], 0, 0)),
            pl.BlockSpec((1, 1, d_model), lambda i, be, na: (be[i], 0, 0)),
        ],
        out_specs=pl.BlockSpec((bm * yp, LANES), lambda i, be, na: (i, 0)),
        scratch_shapes=[pltpu.VMEM((2, bm * xp, LANES), packed.dtype),
                        pltpu.SemaphoreType.DMA((2,))],
    )
    ys = pl.pallas_call(
        functools.partial(_expert_kernel, bm=bm, half=half, fdim=fdim, xp=xp, yp=yp),
        grid_spec=grid_spec,
        out_shape=jax.ShapeDtypeStruct((n_blocks * bm * yp, LANES), F32),
        compiler_params=_cparams("arbitrary"),
        name="moe_experts",
    )(block_e, n_active, row_tok, row_tok, packed, w1, b1.reshape(n_exp, 1, two_f), w2,
      b2.reshape(n_exp, 1, d_model))
    return ys, yp


def _combine_kernel(dest_ref, nxt_ref, y_ref, wts_ref, x_ref, g_ref, o_ref, buf, sems,
                    *, tm, yp, chunks, n_steps):
    i = pl.program_id(0)

    def slab_copy(dests, r, k, slot):
        src0 = pl.multiple_of(dests[r * TOP_K + k] * yp, SUBLANES)
        dst0 = pl.multiple_of(r * yp, SUBLANES)
        return pltpu.make_async_copy(y_ref.at[pl.ds(src0, chunks), :],
                                     buf.at[slot, k, pl.ds(dst0, chunks), :], sems.at[slot])

    def gather_start(dests, slot):
        def issue(g, carry):
            for u in range(DMA_UNROLL // TOP_K):
                for k in range(TOP_K):
                    slab_copy(dests, g * (DMA_UNROLL // TOP_K) + u, k, slot).start(priority=k % 2)
            return carry
        lax.fori_loop(0, tm * TOP_K // DMA_UNROLL, issue, 0)

    def gather_wait(dests, slot):
        def drain(g, carry):
            for u in range(DMA_UNROLL // TOP_K):
                for k in range(TOP_K):
                    slab_copy(dests, g * (DMA_UNROLL // TOP_K) + u, k, slot).wait()
            return carry
        lax.fori_loop(0, tm * TOP_K // DMA_UNROLL, drain, 0)

    def compute(slot):
        w = wts_ref[...]
        wk = [jnp.broadcast_to(w[:, k:k + 1], (tm, LANES)) for k in range(TOP_K)]
        for c in range(chunks):
            acc = wk[0] * buf[slot, 0, pl.ds(c, tm, stride=yp), :]
            for k in range(1, TOP_K):
                acc = acc + wk[k] * buf[slot, k, pl.ds(c, tm, stride=yp), :]
            sl = slice(c * LANES, (c + 1) * LANES)
            o_ref[:, sl] = x_ref[:, sl] + g_ref[0, :, sl] * acc

    @pl.when(i == 0)
    def _first():
        gather_start(dest_ref, 0)

    for slot in range(2):
        @pl.when((i % 2 == slot) & (i + 1 < n_steps))
        def _prefetch(slot=slot):
            gather_start(nxt_ref, 1 - slot)

        @pl.when(i % 2 == slot)
        def _active(slot=slot):
            gather_wait(dest_ref, slot)
            compute(slot)


def _combine(dest, ys, yp, wts, x2d, gate, *, seq, t):
    n_tok, d_model = x2d.shape
    tm = t["cmb_tm"]
    tpb = seq // tm
    chunks = d_model // LANES
    n_steps = n_tok // tm
    return pl.pallas_call(
        functools.partial(_combine_kernel, tm=tm, yp=yp, chunks=chunks, n_steps=n_steps),
        grid=(n_steps,),
        in_specs=[
            pl.BlockSpec((tm * TOP_K,), lambda i: (i,), memory_space=pltpu.SMEM),
            pl.BlockSpec((tm * TOP_K,), lambda i: (jnp.minimum(i + 1, n_steps - 1),),
                         memory_space=pltpu.SMEM),
            pl.BlockSpec(memory_space=pl.ANY),
            pl.BlockSpec((tm, LANES), lambda i: (i, 0)),
            pl.BlockSpec((tm, d_model), lambda i: (i, 0)),
            pl.BlockSpec((1, 1, d_model), lambda i: (i // tpb, 0, 0)),
        ],
        out_specs=pl.BlockSpec((tm, d_model), lambda i: (i, 0)),
        out_shape=jax.ShapeDtypeStruct((n_tok, d_model), F32),
        scratch_shapes=[pltpu.VMEM((2, TOP_K, tm * yp, LANES), F32), pltpu.SemaphoreType.DMA((2,))],
        compiler_params=_cparams("arbitrary"),
        name="moe_combine_residual",
    )(dest, dest, ys, wts, x2d, gate)


def _layer(x2d, mod, norm1_g, w_in, forget_b, pool_w, pool_scale, q_norm_g, k_norm_g, w_out,
           norm2_g, router_w, router_b, w1, b1, w2, b2, *, seq, t):
    n_tok, d_model = x2d.shape
    batch = n_tok // seq
    n_groups, gd, _ = pool_w.shape
    pool_width = n_groups * gd
    att_width = d_model - pool_width
    n_heads = att_width // HEAD_DIM
    n_exp = router_w.shape[1]
    assert pool_width == att_width and n_groups == len(POOL_WINDOWS) and gd % HEAD_DIM == 0
    n_main = pool_width + 3 * att_width
    hp = -(-n_heads // BF16_ROWS) * BF16_ROWS

    shift1, scale1, gate1, shift2, scale2, gate2 = [
        m.reshape(batch, 1, d_model) for m in jnp.split(mod, N_MOD, axis=-1)]

    w_main = w_in[:, :n_main].astype(BF16)
    wf_t = jnp.zeros((hp, d_model), BF16).at[:n_heads].set(w_in[:, n_main:].T.astype(BF16))
    fb = jnp.zeros((hp, 1), F32).at[:n_heads, 0].set(forget_b)
    reps = gd // HEAD_DIM
    main, cum = _in_projection(
        x2d, norm1_g.reshape(1, d_model), scale1, shift1, w_main, wf_t, fb,
        pool_w.astype(BF16), pool_scale.reshape(n_groups, 1, gd),
        jnp.tile(q_norm_g, reps).reshape(1, gd), jnp.tile(k_norm_g, reps).reshape(1, gd),
        seq=seq, t=t)
    att = _attention(main, cum, seq=seq, n_heads=n_heads, q_off=pool_width,
                     k_off=pool_width + att_width, v_off=pool_width + 2 * att_width, t=t)
    x1 = _out_projection(main, att, w_out.astype(BF16), x2d, gate1, seq=seq,
                         pool_width=pool_width, t=t)

    rw = jnp.zeros((d_model, LANES), BF16).at[:, :n_exp].set(router_w.astype(BF16))
    rb = jnp.full((1, LANES), -jnp.inf, F32).at[0, :n_exp].set(router_b)
    packed, idx, wts, rank, counts = _router(
        x1, norm2_g.reshape(1, d_model), scale2, shift2, rw, rb, seq=seq, t=t)

    bm = t["moe_bm"]
    n_assign = n_tok * TOP_K
    n_blocks = -(-n_assign // bm) + n_exp
    counts = counts[0, :n_exp].astype(jnp.int32)
    padded = (counts + bm - 1) // bm * bm
    pad_end = jnp.cumsum(padded)
    pad_start = pad_end - padded
    dest = (pad_start[idx[:, :TOP_K]] + rank[:, :TOP_K]).reshape(-1)
    row_tok = jnp.zeros((n_blocks * bm,), jnp.int32).at[dest].set(
        jnp.arange(n_assign, dtype=jnp.int32) // TOP_K,
        unique_indices=True, mode="promise_in_bounds")
    block_start = jnp.arange(n_blocks, dtype=jnp.int32) * bm
    block_e = jnp.minimum(jnp.sum(block_start[:, None] >= pad_end[None, :], axis=1),
                          n_exp - 1).astype(jnp.int32)
    n_active = (pad_end[-1:] // bm).astype(jnp.int32)

    xp = d_model // 2 // LANES
    ys, yp = _experts(block_e, n_active, row_tok, packed, w1.astype(BF16), b1, w2.astype(BF16), b2,
                      bm=bm, xp=xp)
    return _combine(dest, ys, yp, wts, x1, gate2, seq=seq, t=t)


def kernel(x, c, ada_w, ada_b, norm1_g, w_in, forget_b, pool_w, pool_scale, q_norm_g, k_norm_g,
           w_out, norm2_g, router_w, router_b, expert_w1, expert_b1, expert_w2, expert_b2):
    batch, seq, d_model = x.shape
    depth = ada_w.shape[0]
    t = _tiles(batch * seq, seq, d_model)
    mod = _modulation(c, ada_w, ada_b, t["mod_tn"])
    x2d = x.reshape(batch * seq, d_model)
    for l in range(depth):
        x2d = _layer(x2d, mod[l], norm1_g[l], w_in[l], forget_b[l], pool_w[l], pool_scale[l],
                     q_norm_g[l], k_norm_g[l], w_out[l], norm2_g[l], router_w[l], router_b[l],
                     expert_w1[l], expert_b1[l], expert_w2[l], expert_b2[l], seq=seq, t=t)
    return x2d.reshape(batch, seq, d_model)
```

```python
import functools

import jax
import jax.numpy as jnp
from jax import lax
from jax.experimental import pallas as pl
from jax.experimental.pallas import tpu as pltpu

HEAD_DIM = 128
POOL_WINDOWS = (2, 4, 8, 16)
TOP_K = 4
N_MOD = 6
NORM_EPS = 1e-6
SWIGLU_ALPHA = 1.702
SWIGLU_LIMIT = 7.0
LOG2_E = 1.4426950408889634
ATT_LOGIT_SCALE = HEAD_DIM ** -0.5 * LOG2_E

LANES = 128
SUBLANES = 8
BF16_ROWS = 16
DMA_UNROLL = 8
POOL_HALO = 16
VMEM_LIMIT_BYTES = 56 * 2**20

F32 = jnp.float32
BF16 = jnp.bfloat16


def _cparams(*sem):
    return pltpu.CompilerParams(dimension_semantics=sem, vmem_limit_bytes=VMEM_LIMIT_BYTES)


def _tiles(n_tok, seq, d_model):
    return dict(
        mod_tn=min(512, d_model),
        in_tm=min(512, seq),
        norm_rc=min(64, seq),
        att_tq=min(1024, seq),
        att_tk=512,
        att_heads=2,
        out_tm=min(1024, seq),
        out_tn=min(512, d_model),
        rt_tm=min(512, seq),
        moe_bm=min(256, n_tok),
        cmb_tm=min(128, seq),
    )


def _mod_kernel(c_ref, w_ref, b_ref, o_ref):
    c = c_ref[...]
    ca = c / (1.0 + jnp.exp(-c))
    w = w_ref[0].astype(BF16)
    o_ref[0] = jnp.dot(ca.astype(BF16), w, preferred_element_type=F32) + b_ref[0]


def _modulation(c, ada_w, ada_b, tn):
    depth, d_model, n_out = ada_w.shape
    b = c.shape[0]
    c_pad = jnp.zeros((SUBLANES, d_model), F32).at[:b].set(c)
    out = pl.pallas_call(
        _mod_kernel,
        grid=(depth, n_out // tn),
        in_specs=[
            pl.BlockSpec((SUBLANES, d_model), lambda l, j: (0, 0)),
            pl.BlockSpec((1, d_model, tn), lambda l, j: (l, 0, j)),
            pl.BlockSpec((1, 1, tn), lambda l, j: (l, 0, j)),
        ],
        out_specs=pl.BlockSpec((1, SUBLANES, tn), lambda l, j: (l, 0, j)),
        out_shape=jax.ShapeDtypeStruct((depth, SUBLANES, n_out), F32),
        compiler_params=_cparams("arbitrary", "arbitrary"),
        name="adaln_mod",
    )(c_pad, ada_w, ada_b.reshape(depth, 1, n_out))
    return out[:, :b]


def _in_kernel(x_ref, g_ref, sc_ref, sh_ref, w_ref, wf_ref, fb_ref, pw_ref, ps_ref, qg_ref, kg_ref,
               main_ref, cum_ref, h_sc, tail_sc, fc_sc, *, tm, rc, tpb, gd):
    i = pl.program_id(0)
    j = pl.program_id(1)
    first = (i % tpb) == 0

    @pl.when(j == 0)
    def _norm():
        def body(c, carry):
            r0 = pl.multiple_of(c * rc, rc)
            x = x_ref[pl.ds(r0, rc), :]
            ms = jnp.mean(x * x, axis=-1, keepdims=True)
            y = x * lax.rsqrt(ms + NORM_EPS)
            h = (y * g_ref[...]) * (1.0 + sc_ref[0]) + sh_ref[0]
            h_sc[pl.ds(r0, rc), :] = h.astype(BF16)
            return carry
        lax.fori_loop(0, tm // rc, body, 0)

    def proj():
        return jnp.dot(h_sc[...], w_ref[...], preferred_element_type=F32)

    for g, win in enumerate(POOL_WINDOWS):
        @pl.when(j == g)
        def _pool(g=g, win=win):
            u = proj()
            prev = jnp.where(first, 0.0, tail_sc[g])
            s = jnp.concatenate([prev, u], axis=0)
            shift = 1
            while shift < win:
                s = s + pltpu.roll(s, shift, 0)
                shift *= 2
            wsum = s[POOL_HALO:]
            pos = (i % tpb) * tm + lax.broadcasted_iota(jnp.int32, (tm, 1), 0)
            cnt = jnp.minimum(pos + 1, win).astype(F32)
            mixed = wsum / cnt - u
            po = jnp.dot(mixed.astype(BF16), pw_ref[0], preferred_element_type=F32) * ps_ref[0]
            main_ref[...] = po.astype(BF16)
            tail_sc[g] = u[tm - POOL_HALO:]

    def qk_norm(gain_ref, post_scale):
        r = proj()
        for c in range(gd // HEAD_DIM):
            sl = slice(c * HEAD_DIM, (c + 1) * HEAD_DIM)
            rc_ = r[:, sl]
            ms = jnp.mean(rc_ * rc_, axis=-1, keepdims=True)
            normed = (rc_ * lax.rsqrt(ms + NORM_EPS)) * gain_ref[:, sl]
            if post_scale is not None:
                normed = normed * post_scale
            main_ref[:, sl] = normed.astype(BF16)

    @pl.when((j >= 4) & (j < 8))
    def _q():
        qk_norm(qg_ref, ATT_LOGIT_SCALE)

    @pl.when((j >= 8) & (j < 12))
    def _k():
        qk_norm(kg_ref, None)

    @pl.when((j >= 12) & (j < 16))
    def _v():
        main_ref[...] = proj().astype(BF16)

    @pl.when(j == 16)
    def _forget():
        z = lax.dot_general(wf_ref[...], h_sc[...], (((1,), (1,)), ((), ())),
                            preferred_element_type=F32) + fb_ref[...]
        ls = jnp.minimum(z, 0.0) - jnp.log1p(jnp.exp(-jnp.abs(z)))
        p0 = ls.astype(BF16)
        r1 = ls - p0.astype(F32)
        p1 = r1.astype(BF16)
        p2 = (r1 - p1.astype(F32)).astype(BF16)
        rr = lax.broadcasted_iota(jnp.int32, (tm, tm), 0)
        cc = lax.broadcasted_iota(jnp.int32, (tm, tm), 1)
        tri = jnp.where(rr <= cc, 1.0, 0.0).astype(BF16)
        cum = (jnp.dot(p0, tri, preferred_element_type=F32)
               + jnp.dot(p1, tri, preferred_element_type=F32)
               + jnp.dot(p2, tri, preferred_element_type=F32))
        cum = cum + jnp.where(first, 0.0, fc_sc[:, 0:1])
        cum_ref[0] = cum * LOG2_E
        fc_sc[...] = jnp.broadcast_to(cum[:, tm - 1:tm], fc_sc.shape)


def _in_projection(x2d, gain, scale, shift, w_in_all, layer, n_main, wf_t, fb, pool_w, pool_scale, qg, kg,
                   *, seq, t):
    n_tok, d_model = x2d.shape
    batch = n_tok // seq
    tm, rc = t["in_tm"], t["norm_rc"]
    gd = pool_w.shape[-1]
    hp = wf_t.shape[0]
    tpb = seq // tm
    nj = n_main // gd + 1
    last = n_main // gd - 1
    kern = functools.partial(_in_kernel, tm=tm, rc=rc, tpb=tpb, gd=gd)
    return pl.pallas_call(
        kern,
        grid=(n_tok // tm, nj),
        in_specs=[
            pl.BlockSpec((tm, d_model), lambda i, j: (i, 0)),
            pl.BlockSpec((1, d_model), lambda i, j: (0, 0)),
            pl.BlockSpec((1, 1, d_model), lambda i, j: (i // tpb, 0, 0)),
            pl.BlockSpec((1, 1, d_model), lambda i, j: (i // tpb, 0, 0)),
            pl.BlockSpec((None, d_model, gd), lambda i, j: (layer, 0, jnp.minimum(j, last))),
            pl.BlockSpec((hp, d_model), lambda i, j: (0, 0)),
            pl.BlockSpec((hp, 1), lambda i, j: (0, 0)),
            pl.BlockSpec((1, gd, gd), lambda i, j: (jnp.minimum(j, 3), 0, 0)),
            pl.BlockSpec((1, 1, gd), lambda i, j: (jnp.minimum(j, 3), 0, 0)),
            pl.BlockSpec((1, gd), lambda i, j: (0, 0)),
            pl.BlockSpec((1, gd), lambda i, j: (0, 0)),
        ],
        out_specs=[
            pl.BlockSpec((tm, gd), lambda i, j: (i, jnp.minimum(j, last))),
            pl.BlockSpec((1, hp, tm), lambda i, j: (i // tpb, 0, i % tpb)),
        ],
        out_shape=[
            jax.ShapeDtypeStruct((n_tok, n_main), BF16),
            jax.ShapeDtypeStruct((batch, hp, seq), F32),
        ],
        scratch_shapes=[
            pltpu.VMEM((tm, d_model), BF16),
            pltpu.VMEM((len(POOL_WINDOWS), POOL_HALO, gd), F32),
            pltpu.VMEM((hp, LANES), F32),
        ],
        compiler_params=_cparams("arbitrary", "arbitrary"),
        name="norm_in_proj",
    )(x2d, gain, scale, shift, w_in_all, wf_t, fb, pool_w, pool_scale, qg, kg)


def _attn_kernel(q_ref, k_ref, v_ref, c_ref, o_ref, *, tq, tk, hps):
    qi = pl.program_id(2)
    heads = [slice(h * HEAD_DIM, (h + 1) * HEAD_DIM) for h in range(hps)]
    kpq = tq // tk

    def block(h, kj, carry, diag_index):
        m_prev, l_prev, acc_prev = carry
        start = pl.multiple_of(kj * tk, tk)
        k = k_ref[pl.ds(start, tk), heads[h]]
        v = v_ref[pl.ds(start, tk), heads[h]]
        s = lax.dot_general(q_ref[:, heads[h]], k, (((1,), (1,)), ((), ())),
                            preferred_element_type=F32)
        s = s - c_ref[h, :, pl.ds(start, tk)]
        if diag_index is not None:
            rr = lax.broadcasted_iota(jnp.int32, (tq, tk), 0)
            cc = lax.broadcasted_iota(jnp.int32, (tq, tk), 1)
            s = jnp.where(cc + diag_index * tk <= rr, s, -jnp.inf)
        m_new = jnp.maximum(m_prev, jnp.max(s, axis=-1, keepdims=True))
        p = jnp.exp2(s - m_new)
        alpha = jnp.exp2(m_prev - m_new)
        l_new = alpha * l_prev + jnp.sum(p, axis=-1, keepdims=True)
        acc_new = alpha * acc_prev + jnp.dot(p.astype(BF16), v, preferred_element_type=F32)
        return m_new, l_new, acc_new

    def full_blocks(g, carry):
        for d in range(kpq):
            carry = tuple(block(h, g * kpq + d, carry[h], None) for h in range(hps))
        return carry

    def diagonal_blocks(_, carry):
        for d in range(kpq):
            carry = tuple(block(h, qi * kpq + d, carry[h], d) for h in range(hps))
        return carry

    init = tuple((jnp.full((tq, 1), -jnp.inf, F32), jnp.zeros((tq, 1), F32),
                  jnp.zeros((tq, HEAD_DIM), F32)) for _ in range(hps))
    carry = lax.fori_loop(0, qi, full_blocks, init)
    carry = lax.fori_loop(0, jnp.minimum(qi + 1, 1), diagonal_blocks, carry)
    for h in range(hps):
        _, l_f, acc_f = carry[h]
        o_ref[:, heads[h]] = (acc_f / l_f).astype(o_ref.dtype)


def _attention(main, cum, *, seq, n_heads, q_off, k_off, v_off, t):
    n_tok = main.shape[0]
    batch = n_tok // seq
    tq, hps = t["att_tq"], min(t["att_heads"], n_heads)
    nq = seq // tq
    hp = cum.shape[1]
    width = hps * HEAD_DIM
    assert n_heads % hps == 0 and hp % hps == 0
    cum3 = cum.reshape(batch * hp, 1, seq)
    kern = functools.partial(_attn_kernel, tq=tq, tk=min(t["att_tk"], tq), hps=hps)
    qb, kb, vb, cb = q_off // width, k_off // width, v_off // width, hp // hps
    return pl.pallas_call(
        kern,
        grid=(batch, n_heads // hps, nq),
        in_specs=[
            pl.BlockSpec((tq, width), lambda b, h, i: (b * nq + i, qb + h)),
            pl.BlockSpec((seq, width), lambda b, h, i: (b, kb + h)),
            pl.BlockSpec((seq, width), lambda b, h, i: (b, vb + h)),
            pl.BlockSpec((hps, 1, seq), lambda b, h, i: (b * cb + h, 0, 0)),
        ],
        out_specs=pl.BlockSpec((tq, width), lambda b, h, i: (b * nq + i, h)),
        out_shape=jax.ShapeDtypeStruct((n_tok, n_heads * HEAD_DIM), BF16),
        compiler_params=_cparams("arbitrary", "arbitrary", "arbitrary"),
        name="forget_attention",
    )(main, main, main, cum3)


def _out_kernel(po_ref, at_ref, wp_ref, wa_ref, x_ref, g_ref, o_ref):
    mix = (jnp.dot(po_ref[...], wp_ref[...], preferred_element_type=F32)
           + jnp.dot(at_ref[...], wa_ref[...], preferred_element_type=F32))
    o_ref[...] = x_ref[...] + g_ref[0] * mix


def _out_projection(main, att, w_out_all, layer, x2d, gate, *, seq, pool_width, t):
    n_tok, d_model = x2d.shape
    tm, tn = t["out_tm"], t["out_tn"]
    att_width = att.shape[1]
    tpb = seq // tm
    return pl.pallas_call(
        _out_kernel,
        grid=(n_tok // tm, d_model // tn),
        in_specs=[
            pl.BlockSpec((tm, pool_width), lambda i, j: (i, 0)),
            pl.BlockSpec((tm, att_width), lambda i, j: (i, 0)),
            pl.BlockSpec((None, pool_width, tn), lambda i, j: (layer, 0, j)),
            pl.BlockSpec((None, att_width, tn), lambda i, j: (layer, pool_width // att_width, j)),
            pl.BlockSpec((tm, tn), lambda i, j: (i, j)),
            pl.BlockSpec((1, 1, tn), lambda i, j: (i // tpb, 0, j)),
        ],
        out_specs=pl.BlockSpec((tm, tn), lambda i, j: (i, j)),
        out_shape=jax.ShapeDtypeStruct((n_tok, d_model), F32),
        compiler_params=_cparams("arbitrary", "arbitrary"),
        name="out_proj_residual",
    )(main, att, w_out_all, w_out_all, x2d, gate)


def _router_kernel(x_ref, g_ref, sc_ref, sh_ref, rw_ref, rb_ref,
                   hp_ref, idx_ref, wts_ref, rank_ref, cnt_ref, h_sc, cnt_sc, *, tm, rc, half):
    i = pl.program_id(0)

    @pl.when(i == 0)
    def _init():
        cnt_sc[...] = jnp.zeros(cnt_sc.shape, F32)

    def body(c, carry):
        r0 = pl.multiple_of(c * rc, rc)
        x = x_ref[pl.ds(r0, rc), :]
        ms = jnp.mean(x * x, axis=-1, keepdims=True)
        y = x * lax.rsqrt(ms + NORM_EPS)
        h = ((y * g_ref[...]) * (1.0 + sc_ref[0]) + sh_ref[0]).astype(BF16)
        h_sc[pl.ds(r0, rc), :] = h
        bits = lax.bitcast_convert_type(h.astype(F32), jnp.uint32)
        lo = lax.shift_right_logical(bits[:, :half], jnp.uint32(16))
        hi = bits[:, half:] & jnp.uint32(0xFFFF0000)
        words = lo | hi
        pitch = half // LANES
        for s in range(pitch):
            hp_ref[pl.ds(r0 * pitch + s, rc, stride=pitch), :] = words[:, s * LANES:(s + 1) * LANES]
        return carry
    lax.fori_loop(0, tm // rc, body, 0)

    logits = jnp.dot(h_sc[...], rw_ref[...], preferred_element_type=F32) + rb_ref[...]
    lane = lax.broadcasted_iota(jnp.int32, logits.shape, 1).astype(F32)
    vals, sels, hots = [], [], []
    cur = logits
    for _ in range(TOP_K):
        mx = jnp.max(cur, axis=-1, keepdims=True)
        sel = jnp.min(jnp.where(cur == mx, lane, float(LANES)), axis=-1, keepdims=True)
        hot = lane == sel
        vals.append(mx)
        sels.append(sel)
        hots.append(hot)
        cur = jnp.where(hot, -jnp.inf, cur)
    exps = [jnp.exp(v - vals[0]) for v in vals]
    denom = exps[0] + exps[1] + exps[2] + exps[3]

    hot_sum = jnp.zeros(logits.shape, F32)
    for hot in hots:
        hot_sum = hot_sum + jnp.where(hot, 1.0, 0.0)
    rr = lax.broadcasted_iota(jnp.int32, (tm, tm), 0)
    cc = lax.broadcasted_iota(jnp.int32, (tm, tm), 1)
    below = jnp.where(cc < rr, 1.0, 0.0).astype(BF16)
    base = jnp.dot(below, hot_sum.astype(BF16), preferred_element_type=F32) + cnt_sc[...]

    idx_out = jnp.zeros(logits.shape, F32)
    wts_out = jnp.zeros(logits.shape, F32)
    rank_out = jnp.zeros(logits.shape, F32)
    for k in range(TOP_K):
        col = lane == float(k)
        rank_k = jnp.sum(jnp.where(hots[k], base, 0.0), axis=-1, keepdims=True)
        idx_out = jnp.where(col, sels[k], idx_out)
        wts_out = jnp.where(col, exps[k] / denom, wts_out)
        rank_out = jnp.where(col, rank_k, rank_out)
    idx_ref[...] = idx_out.astype(jnp.int32)
    wts_ref[...] = wts_out
    rank_ref[...] = rank_out.astype(jnp.int32)
    total = cnt_sc[...] + jnp.sum(hot_sum, axis=0, keepdims=True)
    cnt_sc[...] = total
    cnt_ref[...] = total


def _router(x2d, gain, scale, shift, rw, rb, *, seq, t):
    n_tok, d_model = x2d.shape
    tm, rc = t["rt_tm"], t["norm_rc"]
    tpb = seq // tm
    half = d_model // 2
    kern = functools.partial(_router_kernel, tm=tm, rc=rc, half=half)
    return pl.pallas_call(
        kern,
        grid=(n_tok // tm,),
        in_specs=[
            pl.BlockSpec((tm, d_model), lambda i: (i, 0)),
            pl.BlockSpec((1, d_model), lambda i: (0, 0)),
            pl.BlockSpec((1, 1, d_model), lambda i: (i // tpb, 0, 0)),
            pl.BlockSpec((1, 1, d_model), lambda i: (i // tpb, 0, 0)),
            pl.BlockSpec((d_model, LANES), lambda i: (0, 0)),
            pl.BlockSpec((1, LANES), lambda i: (0, 0)),
        ],
        out_specs=[
            pl.BlockSpec((tm * (half // LANES), LANES), lambda i: (i, 0)),
            pl.BlockSpec((tm, LANES), lambda i: (i, 0)),
            pl.BlockSpec((tm, LANES), lambda i: (i, 0)),
            pl.BlockSpec((tm, LANES), lambda i: (i, 0)),
            pl.BlockSpec((1, LANES), lambda i: (0, 0)),
        ],
        out_shape=[
            jax.ShapeDtypeStruct((n_tok * (half // LANES), LANES), jnp.uint32),
            jax.ShapeDtypeStruct((n_tok, LANES), jnp.int32),
            jax.ShapeDtypeStruct((n_tok, LANES), F32),
            jax.ShapeDtypeStruct((n_tok, LANES), jnp.int32),
            jax.ShapeDtypeStruct((1, LANES), F32),
        ],
        scratch_shapes=[pltpu.VMEM((tm, d_model), BF16), pltpu.VMEM((1, LANES), F32)],
        compiler_params=_cparams("arbitrary"),
        name="norm_router_topk",
    )(x2d, gain, scale, shift, rw, rb)


def _expert_kernel(be_ref, na_ref, tok_ref, nxt_ref, src_ref, w1_ref, b1_ref, w2_ref, b2_ref,
                   o_ref, xbuf, sems, *, bm, half, fdim, xp, yp):
    del be_ref
    i = pl.program_id(0)
    n_active = na_ref[0]
    chunks = 2 * half // LANES

    def slab_copy(toks, r, slot):
        src0 = pl.multiple_of(toks[r] * xp, xp)
        dst0 = pl.multiple_of(r * xp, xp)
        return pltpu.make_async_copy(src_ref.at[pl.ds(src0, xp), :],
                                     xbuf.at[slot, pl.ds(dst0, xp), :], sems.at[slot])

    def gather_start(toks, slot):
        def issue(g, carry):
            for u in range(DMA_UNROLL):
                slab_copy(toks, g * DMA_UNROLL + u, slot).start(priority=u % 2)
            return carry
        lax.fori_loop(0, bm // DMA_UNROLL, issue, 0)

    def gather_wait(toks, slot):
        def drain(g, carry):
            for u in range(DMA_UNROLL):
                slab_copy(toks, g * DMA_UNROLL + u, slot).wait()
            return carry
        lax.fori_loop(0, bm // DMA_UNROLL, drain, 0)

    def compute(slot):
        words = jnp.concatenate([xbuf[slot, pl.ds(s, bm, stride=xp), :] for s in range(xp)], axis=1)
        lo = lax.bitcast_convert_type(lax.shift_left(words, jnp.uint32(16)), F32).astype(BF16)
        hi = lax.bitcast_convert_type(words & jnp.uint32(0xFFFF0000), F32).astype(BF16)
        hid = (jnp.dot(lo, w1_ref[0, :half, :], preferred_element_type=F32)
               + jnp.dot(hi, w1_ref[0, half:, :], preferred_element_type=F32)) + b1_ref[0]
        glu = jnp.minimum(hid[:, :fdim], SWIGLU_LIMIT)
        lin = jnp.clip(hid[:, fdim:], -SWIGLU_LIMIT, SWIGLU_LIMIT)
        act = (glu / (1.0 + jnp.exp(-SWIGLU_ALPHA * glu))) * (lin + 1.0)
        y = jnp.dot(act.astype(BF16), w2_ref[0], preferred_element_type=F32) + b2_ref[0]
        for c in range(chunks):
            o_ref[pl.ds(c, bm, stride=yp), :] = y[:, c * LANES:(c + 1) * LANES]
        for c in range(chunks, yp):
            o_ref[pl.ds(c, bm, stride=yp), :] = jnp.zeros((bm, LANES), F32)

    @pl.when(i == 0)
    def _first():
        gather_start(tok_ref, 0)

    for slot in range(2):
        @pl.when((i % 2 == slot) & (i + 1 < n_active))
        def _prefetch(slot=slot):
            gather_start(nxt_ref, 1 - slot)

        @pl.when((i % 2 == slot) & (i < n_active))
        def _active(slot=slot):
            gather_wait(tok_ref, slot)
            compute(slot)

    @pl.when(i >= n_active)
    def _unused():
        o_ref[...] = jnp.zeros(o_ref.shape, o_ref.dtype)


def _slab_pitch(chunks):
    groups = -(-chunks // SUBLANES)
    return (groups + 1 - groups % 2) * SUBLANES


def _experts(block_e, n_active, row_tok, packed, w1_all, b1_all, w2_all, b2_all, layer, *, bm, xp):
    n_blocks = row_tok.shape[0] // bm
    depth, n_exp, d_model, two_f = w1_all.shape
    fdim, half = two_f // 2, d_model // 2
    yp = _slab_pitch(d_model // LANES)
    grid_spec = pltpu.PrefetchScalarGridSpec(
        num_scalar_prefetch=2,
        grid=(n_blocks,),
        in_specs=[
            pl.BlockSpec((bm,), lambda i, be, na: (i,), memory_space=pltpu.SMEM),
            pl.BlockSpec((bm,), lambda i, be, na: (jnp.minimum(i + 1, n_blocks - 1),),
                         memory_space=pltpu.SMEM),
            pl.BlockSpec(memory_space=pl.ANY),
            pl.BlockSpec((None, 1, d_model, two_f), lambda i, be, na: (layer, be[i], 0, 0)),
            pl.BlockSpec((None, 1, 1, two_f), lambda i, be, na: (layer, be[i], 0, 0)),
            pl.BlockSpec((None, 1, fdim, d_model), lambda i, be, na: (layer, be[i], 0, 0)),
            pl.BlockSpec((None, 1, 1, d_model), lambda i, be, na: (layer, be[i], 0, 0)),
        ],
        out_specs=pl.BlockSpec((bm * yp, LANES), lambda i, be, na: (i, 0)),
        scratch_shapes=[pltpu.VMEM((2, bm * xp, LANES), packed.dtype),
                        pltpu.SemaphoreType.DMA((2,))],
    )
    ys = pl.pallas_call(
        functools.partial(_expert_kernel, bm=bm, half=half, fdim=fdim, xp=xp, yp=yp),
        grid_spec=grid_spec,
        out_shape=jax.ShapeDtypeStruct((n_blocks * bm * yp, LANES), F32),
        compiler_params=_cparams("arbitrary"),
        name="moe_experts",
    )(block_e, n_active, row_tok, row_tok, packed, w1_all, b1_all.reshape(depth, n_exp, 1, two_f),
      w2_all, b2_all.reshape(depth, n_exp, 1, d_model))
    return ys, yp


def _combine_kernel(dest_ref, nxt_ref, y_ref, wts_ref, x_ref, g_ref, o_ref, buf, sems,
                    *, tm, yp, chunks, n_steps):
    i = pl.program_id(0)

    def slab_copy(dests, r, k, slot):
        src0 = pl.multiple_of(dests[r * TOP_K + k] * yp, SUBLANES)
        dst0 = pl.multiple_of(r * yp, SUBLANES)
        return pltpu.make_async_copy(y_ref.at[pl.ds(src0, chunks), :],
                                     buf.at[slot, k, pl.ds(dst0, chunks), :], sems.at[slot])

    def gather_start(dests, slot):
        def issue(g, carry):
            for u in range(DMA_UNROLL // TOP_K):
                for k in range(TOP_K):
                    slab_copy(dests, g * (DMA_UNROLL // TOP_K) + u, k, slot).start(priority=k % 2)
            return carry
        lax.fori_loop(0, tm * TOP_K // DMA_UNROLL, issue, 0)

    def gather_wait(dests, slot):
        def drain(g, carry):
            for u in range(DMA_UNROLL // TOP_K):
                for k in range(TOP_K):
                    slab_copy(dests, g * (DMA_UNROLL // TOP_K) + u, k, slot).wait()
            return carry
        lax.fori_loop(0, tm * TOP_K // DMA_UNROLL, drain, 0)

    def compute(slot):
        w = wts_ref[...]
        wk = [jnp.broadcast_to(w[:, k:k + 1], (tm, LANES)) for k in range(TOP_K)]
        for c in range(chunks):
            acc = wk[0] * buf[slot, 0, pl.ds(c, tm, stride=yp), :]
            for k in range(1, TOP_K):
                acc = acc + wk[k] * buf[slot, k, pl.ds(c, tm, stride=yp), :]
            sl = slice(c * LANES, (c + 1) * LANES)
            o_ref[:, sl] = x_ref[:, sl] + g_ref[0, :, sl] * acc

    @pl.when(i == 0)
    def _first():
        gather_start(dest_ref, 0)

    for slot in range(2):
        @pl.when((i % 2 == slot) & (i + 1 < n_steps))
        def _prefetch(slot=slot):
            gather_start(nxt_ref, 1 - slot)

        @pl.when(i % 2 == slot)
        def _active(slot=slot):
            gather_wait(dest_ref, slot)
            compute(slot)


def _combine(dest, ys, yp, wts, x2d, gate, *, seq, t):
    n_tok, d_model = x2d.shape
    tm = t["cmb_tm"]
    tpb = seq // tm
    chunks = d_model // LANES
    n_steps = n_tok // tm
    return pl.pallas_call(
        functools.partial(_combine_kernel, tm=tm, yp=yp, chunks=chunks, n_steps=n_steps),
        grid=(n_steps,),
        in_specs=[
            pl.BlockSpec((tm * TOP_K,), lambda i: (i,), memory_space=pltpu.SMEM),
            pl.BlockSpec((tm * TOP_K,), lambda i: (jnp.minimum(i + 1, n_steps - 1),),
                         memory_space=pltpu.SMEM),
            pl.BlockSpec(memory_space=pl.ANY),
            pl.BlockSpec((tm, LANES), lambda i: (i, 0)),
            pl.BlockSpec((tm, d_model), lambda i: (i, 0)),
            pl.BlockSpec((1, 1, d_model), lambda i: (i // tpb, 0, 0)),
        ],
        out_specs=pl.BlockSpec((tm, d_model), lambda i: (i, 0)),
        out_shape=jax.ShapeDtypeStruct((n_tok, d_model), F32),
        scratch_shapes=[pltpu.VMEM((2, TOP_K, tm * yp, LANES), F32), pltpu.SemaphoreType.DMA((2,))],
        compiler_params=_cparams("arbitrary"),
        name="moe_combine_residual",
    )(dest, dest, ys, wts, x2d, gate)


def _layer(layer, x2d, mod, norm1_g, w_in_f, w_in_all, forget_b, pool_w, pool_scale, q_norm_g, k_norm_g,
           w_out_all, norm2_g, router_w, router_b, w1_all, b1_all, w2_all, b2_all, *, seq, t):
    n_tok, d_model = x2d.shape
    batch = n_tok // seq
    n_groups, gd, _ = pool_w.shape
    pool_width = n_groups * gd
    att_width = d_model - pool_width
    n_heads = att_width // HEAD_DIM
    n_exp = router_w.shape[1]
    assert pool_width == att_width and n_groups == len(POOL_WINDOWS) and gd % HEAD_DIM == 0
    n_main = pool_width + 3 * att_width
    hp = -(-n_heads // BF16_ROWS) * BF16_ROWS

    shift1, scale1, gate1, shift2, scale2, gate2 = [
        m.reshape(batch, 1, d_model) for m in jnp.split(mod, N_MOD, axis=-1)]

    wf_src = lax.optimization_barrier(w_in_f[:, n_main:])
    wf_t = jnp.zeros((hp, d_model), BF16).at[:n_heads].set(wf_src.T.astype(BF16))
    fb = jnp.zeros((hp, 1), F32).at[:n_heads, 0].set(forget_b)
    reps = gd // HEAD_DIM
    main, cum = _in_projection(
        x2d, norm1_g.reshape(1, d_model), scale1, shift1, w_in_all, layer, n_main, wf_t, fb,
        pool_w.astype(BF16), pool_scale.reshape(n_groups, 1, gd),
        jnp.tile(q_norm_g, reps).reshape(1, gd), jnp.tile(k_norm_g, reps).reshape(1, gd),
        seq=seq, t=t)
    att = _attention(main, cum, seq=seq, n_heads=n_heads, q_off=pool_width,
                     k_off=pool_width + att_width, v_off=pool_width + 2 * att_width, t=t)
    x1 = _out_projection(main, att, w_out_all, layer, x2d, gate1, seq=seq,
                         pool_width=pool_width, t=t)

    rw = jnp.zeros((d_model, LANES), BF16).at[:, :n_exp].set(router_w.astype(BF16))
    rb = jnp.full((1, LANES), -jnp.inf, F32).at[0, :n_exp].set(router_b)
    packed, idx, wts, rank, counts = _router(
        x1, norm2_g.reshape(1, d_model), scale2, shift2, rw, rb, seq=seq, t=t)

    bm = t["moe_bm"]
    n_assign = n_tok * TOP_K
    n_blocks = -(-n_assign // bm) + n_exp
    counts = counts[0, :n_exp].astype(jnp.int32)
    padded = (counts + bm - 1) // bm * bm
    pad_end = jnp.cumsum(padded)
    pad_start = pad_end - padded
    dest = (pad_start[idx[:, :TOP_K]] + rank[:, :TOP_K]).reshape(-1)
    row_tok = jnp.zeros((n_blocks * bm,), jnp.int32).at[dest].set(
        jnp.arange(n_assign, dtype=jnp.int32) // TOP_K,
        unique_indices=True, mode="promise_in_bounds")
    block_start = jnp.arange(n_blocks, dtype=jnp.int32) * bm
    block_e = jnp.minimum(jnp.sum(block_start[:, None] >= pad_end[None, :], axis=1),
                          n_exp - 1).astype(jnp.int32)
    n_active = (pad_end[-1:] // bm).astype(jnp.int32)

    xp = d_model // 2 // LANES
    ys, yp = _experts(block_e, n_active, row_tok, packed, w1_all, b1_all, w2_all, b2_all, layer,
                      bm=bm, xp=xp)
    return _combine(dest, ys, yp, wts, x1, gate2, seq=seq, t=t)


def kernel(x, c, ada_w, ada_b, norm1_g, w_in, forget_b, pool_w, pool_scale, q_norm_g, k_norm_g,
           w_out, norm2_g, router_w, router_b, expert_w1, expert_b1, expert_w2, expert_b2):
    batch, seq, d_model = x.shape
    depth = ada_w.shape[0]
    t = _tiles(batch * seq, seq, d_model)
    mod = _modulation(c, ada_w, ada_b, t["mod_tn"])
    x2d = x.reshape(batch * seq, d_model)
    w_in_all, w_out_all = w_in.astype(BF16), w_out.astype(BF16)
    w1_all, w2_all = expert_w1.astype(BF16), expert_w2.astype(BF16)
    for l in range(depth):
        x2d = _layer(l, x2d, mod[l], norm1_g[l], w_in[l], w_in_all, forget_b[l], pool_w[l], pool_scale[l],
                     q_norm_g[l], k_norm_g[l], w_out_all, norm2_g[l], router_w[l], router_b[l],
                     w1_all, expert_b1, w2_all, expert_b2, seq=seq, t=t)
    return x2d.reshape(batch, seq, d_model)
```

```python
import functools

import jax
import jax.numpy as jnp
from jax import lax
from jax.experimental import pallas as pl
from jax.experimental.pallas import tpu as pltpu

HEAD_DIM = 128
POOL_WINDOWS = (2, 4, 8, 16)
TOP_K = 4
N_MOD = 6
NORM_EPS = 1e-6
SWIGLU_ALPHA = 1.702
SWIGLU_LIMIT = 7.0
LOG2_E = 1.4426950408889634
ATT_LOGIT_SCALE = HEAD_DIM ** -0.5 * LOG2_E

LANES = 128
SUBLANES = 8
BF16_ROWS = 16
DMA_UNROLL = 8
POOL_HALO = 16
VMEM_LIMIT_BYTES = 56 * 2**20

F32 = jnp.float32
BF16 = jnp.bfloat16


def _cparams(*sem):
    return pltpu.CompilerParams(dimension_semantics=sem, vmem_limit_bytes=VMEM_LIMIT_BYTES)


def _tiles(n_tok, seq, d_model):
    return dict(
        mod_tn=min(512, d_model),
        norm_tm=min(512, seq),
        in_tm=min(1024, seq),
        norm_rc=min(64, seq),
        att_tq=min(1024, seq),
        att_tk=512,
        att_heads=2,
        out_tm=min(1024, seq),
        out_tn=min(512, d_model),
        rt_tm=min(512, seq),
        moe_bm=min(512, n_tok),
        cmb_tm=min(128, seq),
    )


def _mod_kernel(c_ref, w_ref, b_ref, o_ref):
    c = c_ref[...]
    ca = c / (1.0 + jnp.exp(-c))
    w = w_ref[0].astype(BF16)
    o_ref[0] = jnp.dot(ca.astype(BF16), w, preferred_element_type=F32) + b_ref[0]


def _modulation(c, ada_w, ada_b, tn):
    depth, d_model, n_out = ada_w.shape
    b = c.shape[0]
    c_pad = jnp.zeros((SUBLANES, d_model), F32).at[:b].set(c)
    out = pl.pallas_call(
        _mod_kernel,
        grid=(depth, n_out // tn),
        in_specs=[
            pl.BlockSpec((SUBLANES, d_model), lambda l, j: (0, 0)),
            pl.BlockSpec((1, d_model, tn), lambda l, j: (l, 0, j)),
            pl.BlockSpec((1, 1, tn), lambda l, j: (l, 0, j)),
        ],
        out_specs=pl.BlockSpec((1, SUBLANES, tn), lambda l, j: (l, 0, j)),
        out_shape=jax.ShapeDtypeStruct((depth, SUBLANES, n_out), F32),
        compiler_params=_cparams("arbitrary", "arbitrary"),
        name="adaln_mod",
    )(c_pad, ada_w, ada_b.reshape(depth, 1, n_out))
    return out[:, :b]


def _norm_kernel(x_ref, g_ref, sc_ref, sh_ref, h_ref, *, tm, rc):
    def body(c, carry):
        r0 = pl.multiple_of(c * rc, rc)
        x = x_ref[pl.ds(r0, rc), :]
        ms = jnp.mean(x * x, axis=-1, keepdims=True)
        y = x * lax.rsqrt(ms + NORM_EPS)
        h = (y * g_ref[...]) * (1.0 + sc_ref[0]) + sh_ref[0]
        h_ref[pl.ds(r0, rc), :] = h.astype(BF16)
        return carry
    lax.fori_loop(0, tm // rc, body, 0)


def _norm_modulate(x2d, gain, scale, shift, *, seq, t):
    n_tok, d_model = x2d.shape
    tm, rc = t["norm_tm"], t["norm_rc"]
    tpb = seq // tm
    return pl.pallas_call(
        functools.partial(_norm_kernel, tm=tm, rc=rc),
        grid=(n_tok // tm,),
        in_specs=[
            pl.BlockSpec((tm, d_model), lambda i: (i, 0)),
            pl.BlockSpec((1, d_model), lambda i: (0, 0)),
            pl.BlockSpec((1, 1, d_model), lambda i: (i // tpb, 0, 0)),
            pl.BlockSpec((1, 1, d_model), lambda i: (i // tpb, 0, 0)),
        ],
        out_specs=pl.BlockSpec((tm, d_model), lambda i: (i, 0)),
        out_shape=jax.ShapeDtypeStruct((n_tok, d_model), BF16),
        compiler_params=_cparams("arbitrary"),
        name="norm_modulate",
    )(x2d, gain, scale, shift)


def _in_kernel(h_ref, w_ref, wf_ref, fb_ref, pw_ref, ps_ref, qg_ref, kg_ref,
               main_ref, cum_ref, tail_sc, fc_sc, *, tm, tpb, gd):
    i = pl.program_id(0)
    j = pl.program_id(1)
    first = (i % tpb) == 0

    def proj():
        return jnp.dot(h_ref[...], w_ref[...], preferred_element_type=F32)

    for g, win in enumerate(POOL_WINDOWS):
        @pl.when(j == g)
        def _pool(g=g, win=win):
            u = proj()
            prev = jnp.where(first, 0.0, tail_sc[g])
            s = jnp.concatenate([prev, u], axis=0)
            shift = 1
            while shift < win:
                s = s + pltpu.roll(s, shift, 0)
                shift *= 2
            wsum = s[POOL_HALO:]
            pos = (i % tpb) * tm + lax.broadcasted_iota(jnp.int32, (tm, 1), 0)
            cnt = jnp.minimum(pos + 1, win).astype(F32)
            mixed = wsum / cnt - u
            po = jnp.dot(mixed.astype(BF16), pw_ref[0], preferred_element_type=F32) * ps_ref[0]
            main_ref[...] = po.astype(BF16)
            tail_sc[g] = u[tm - POOL_HALO:]

    def qk_norm(gain_ref, post_scale):
        r = proj()
        for c in range(gd // HEAD_DIM):
            sl = slice(c * HEAD_DIM, (c + 1) * HEAD_DIM)
            rc_ = r[:, sl]
            ms = jnp.mean(rc_ * rc_, axis=-1, keepdims=True)
            normed = (rc_ * lax.rsqrt(ms + NORM_EPS)) * gain_ref[:, sl]
            if post_scale is not None:
                normed = normed * post_scale
            main_ref[:, sl] = normed.astype(BF16)

    @pl.when((j >= 4) & (j < 8))
    def _q():
        qk_norm(qg_ref, ATT_LOGIT_SCALE)

    @pl.when((j >= 8) & (j < 12))
    def _k():
        qk_norm(kg_ref, None)

    @pl.when((j >= 12) & (j < 16))
    def _v():
        main_ref[...] = proj().astype(BF16)

    @pl.when(j == 16)
    def _forget():
        z = lax.dot_general(wf_ref[...], h_ref[...], (((1,), (1,)), ((), ())),
                            preferred_element_type=F32) + fb_ref[...]
        ls = jnp.minimum(z, 0.0) - jnp.log1p(jnp.exp(-jnp.abs(z)))
        p0 = ls.astype(BF16)
        r1 = ls - p0.astype(F32)
        p1 = r1.astype(BF16)
        p2 = (r1 - p1.astype(F32)).astype(BF16)
        rr = lax.broadcasted_iota(jnp.int32, (tm, tm), 0)
        cc = lax.broadcasted_iota(jnp.int32, (tm, tm), 1)
        tri = jnp.where(rr <= cc, 1.0, 0.0).astype(BF16)
        cum = (jnp.dot(p0, tri, preferred_element_type=F32)
               + jnp.dot(p1, tri, preferred_element_type=F32)
               + jnp.dot(p2, tri, preferred_element_type=F32))
        cum = cum + jnp.where(first, 0.0, fc_sc[:, 0:1])
        cum_ref[0] = cum * LOG2_E
        fc_sc[...] = jnp.broadcast_to(cum[:, tm - 1:tm], fc_sc.shape)


def _in_projection(h2d, w_in_all, layer, n_main, wf_t, fb, pool_w, pool_scale, qg, kg, *, seq, t):
    n_tok, d_model = h2d.shape
    batch = n_tok // seq
    tm = t["in_tm"]
    gd = pool_w.shape[-1]
    hp = wf_t.shape[0]
    tpb = seq // tm
    nj = n_main // gd + 1
    last = n_main // gd - 1
    kern = functools.partial(_in_kernel, tm=tm, tpb=tpb, gd=gd)
    return pl.pallas_call(
        kern,
        grid=(n_tok // tm, nj),
        in_specs=[
            pl.BlockSpec((tm, d_model), lambda i, j: (i, 0)),
            pl.BlockSpec((None, d_model, gd), lambda i, j: (layer, 0, jnp.minimum(j, last))),
            pl.BlockSpec((hp, d_model), lambda i, j: (0, 0)),
            pl.BlockSpec((hp, 1), lambda i, j: (0, 0)),
            pl.BlockSpec((1, gd, gd), lambda i, j: (jnp.minimum(j, 3), 0, 0)),
            pl.BlockSpec((1, 1, gd), lambda i, j: (jnp.minimum(j, 3), 0, 0)),
            pl.BlockSpec((1, gd), lambda i, j: (0, 0)),
            pl.BlockSpec((1, gd), lambda i, j: (0, 0)),
        ],
        out_specs=[
            pl.BlockSpec((tm, gd), lambda i, j: (i, jnp.minimum(j, last))),
            pl.BlockSpec((1, hp, tm), lambda i, j: (i // tpb, 0, i % tpb)),
        ],
        out_shape=[
            jax.ShapeDtypeStruct((n_tok, n_main), BF16),
            jax.ShapeDtypeStruct((batch, hp, seq), F32),
        ],
        scratch_shapes=[
            pltpu.VMEM((len(POOL_WINDOWS), POOL_HALO, gd), F32),
            pltpu.VMEM((hp, LANES), F32),
        ],
        compiler_params=_cparams("arbitrary", "arbitrary"),
        name="in_proj",
    )(h2d, w_in_all, wf_t, fb, pool_w, pool_scale, qg, kg)


def _attn_kernel(q_ref, k_ref, v_ref, c_ref, o_ref, *, tq, tk, hps):
    qi = pl.program_id(2)
    heads = [slice(h * HEAD_DIM, (h + 1) * HEAD_DIM) for h in range(hps)]
    kpq = tq // tk

    def block(h, kj, carry, diag_index):
        m_prev, l_prev, acc_prev = carry
        start = pl.multiple_of(kj * tk, tk)
        k = k_ref[pl.ds(start, tk), heads[h]]
        v = v_ref[pl.ds(start, tk), heads[h]]
        s = lax.dot_general(q_ref[:, heads[h]], k, (((1,), (1,)), ((), ())),
                            preferred_element_type=F32)
        s = s - c_ref[h, :, pl.ds(start, tk)]
        if diag_index is not None:
            rr = lax.broadcasted_iota(jnp.int32, (tq, tk), 0)
            cc = lax.broadcasted_iota(jnp.int32, (tq, tk), 1)
            s = jnp.where(cc + diag_index * tk <= rr, s, -jnp.inf)
        m_new = jnp.maximum(m_prev, jnp.max(s, axis=-1, keepdims=True))
        p = jnp.exp2(s - m_new)
        alpha = jnp.exp2(m_prev - m_new)
        l_new = alpha * l_prev + jnp.sum(p, axis=-1, keepdims=True)
        acc_new = alpha * acc_prev + jnp.dot(p.astype(BF16), v, preferred_element_type=F32)
        return m_new, l_new, acc_new

    def full_blocks(g, carry):
        for d in range(kpq):
            carry = tuple(block(h, g * kpq + d, carry[h], None) for h in range(hps))
        return carry

    def diagonal_blocks(_, carry):
        for d in range(kpq):
            carry = tuple(block(h, qi * kpq + d, carry[h], d) for h in range(hps))
        return carry

    init = tuple((jnp.full((tq, 1), -jnp.inf, F32), jnp.zeros((tq, 1), F32),
                  jnp.zeros((tq, HEAD_DIM), F32)) for _ in range(hps))
    carry = lax.fori_loop(0, qi, full_blocks, init)
    carry = lax.fori_loop(0, jnp.minimum(qi + 1, 1), diagonal_blocks, carry)
    for h in range(hps):
        _, l_f, acc_f = carry[h]
        o_ref[:, heads[h]] = (acc_f / l_f).astype(o_ref.dtype)


def _attention(main, cum, *, seq, n_heads, q_off, k_off, v_off, t):
    n_tok = main.shape[0]
    batch = n_tok // seq
    tq, hps = t["att_tq"], min(t["att_heads"], n_heads)
    nq = seq // tq
    hp = cum.shape[1]
    width = hps * HEAD_DIM
    assert n_heads % hps == 0 and hp % hps == 0
    cum3 = cum.reshape(batch * hp, 1, seq)
    kern = functools.partial(_attn_kernel, tq=tq, tk=min(t["att_tk"], tq), hps=hps)
    qb, kb, vb, cb = q_off // width, k_off // width, v_off // width, hp // hps
    return pl.pallas_call(
        kern,
        grid=(batch, n_heads // hps, nq),
        in_specs=[
            pl.BlockSpec((tq, width), lambda b, h, i: (b * nq + i, qb + h)),
            pl.BlockSpec((seq, width), lambda b, h, i: (b, kb + h)),
            pl.BlockSpec((seq, width), lambda b, h, i: (b, vb + h)),
            pl.BlockSpec((hps, 1, seq), lambda b, h, i: (b * cb + h, 0, 0)),
        ],
        out_specs=pl.BlockSpec((tq, width), lambda b, h, i: (b * nq + i, h)),
        out_shape=jax.ShapeDtypeStruct((n_tok, n_heads * HEAD_DIM), BF16),
        compiler_params=_cparams("arbitrary", "arbitrary", "arbitrary"),
        name="forget_attention",
    )(main, main, main, cum3)


def _out_kernel(po_ref, at_ref, wp_ref, wa_ref, x_ref, g_ref, o_ref):
    mix = (jnp.dot(po_ref[...], wp_ref[...], preferred_element_type=F32)
           + jnp.dot(at_ref[...], wa_ref[...], preferred_element_type=F32))
    o_ref[...] = x_ref[...] + g_ref[0] * mix


def _out_projection(main, att, w_out_all, layer, x2d, gate, *, seq, pool_width, t):
    n_tok, d_model = x2d.shape
    tm, tn = t["out_tm"], t["out_tn"]
    att_width = att.shape[1]
    tpb = seq // tm
    return pl.pallas_call(
        _out_kernel,
        grid=(n_tok // tm, d_model // tn),
        in_specs=[
            pl.BlockSpec((tm, pool_width), lambda i, j: (i, 0)),
            pl.BlockSpec((tm, att_width), lambda i, j: (i, 0)),
            pl.BlockSpec((None, pool_width, tn), lambda i, j: (layer, 0, j)),
            pl.BlockSpec((None, att_width, tn), lambda i, j: (layer, pool_width // att_width, j)),
            pl.BlockSpec((tm, tn), lambda i, j: (i, j)),
            pl.BlockSpec((1, 1, tn), lambda i, j: (i // tpb, 0, j)),
        ],
        out_specs=pl.BlockSpec((tm, tn), lambda i, j: (i, j)),
        out_shape=jax.ShapeDtypeStruct((n_tok, d_model), F32),
        compiler_params=_cparams("arbitrary", "arbitrary"),
        name="out_proj_residual",
    )(main, att, w_out_all, w_out_all, x2d, gate)


def _router_kernel(x_ref, g_ref, sc_ref, sh_ref, rw_ref, rb_ref,
                   hp_ref, idx_ref, wts_ref, rank_ref, cnt_ref, h_sc, cnt_sc, *, tm, rc, half):
    i = pl.program_id(0)

    @pl.when(i == 0)
    def _init():
        cnt_sc[...] = jnp.zeros(cnt_sc.shape, F32)

    def body(c, carry):
        r0 = pl.multiple_of(c * rc, rc)
        x = x_ref[pl.ds(r0, rc), :]
        ms = jnp.mean(x * x, axis=-1, keepdims=True)
        y = x * lax.rsqrt(ms + NORM_EPS)
        h = ((y * g_ref[...]) * (1.0 + sc_ref[0]) + sh_ref[0]).astype(BF16)
        h_sc[pl.ds(r0, rc), :] = h
        bits = lax.bitcast_convert_type(h.astype(F32), jnp.uint32)
        lo = lax.shift_right_logical(bits[:, :half], jnp.uint32(16))
        hi = bits[:, half:] & jnp.uint32(0xFFFF0000)
        words = lo | hi
        pitch = half // LANES
        for s in range(pitch):
            hp_ref[pl.ds(r0 * pitch + s, rc, stride=pitch), :] = words[:, s * LANES:(s + 1) * LANES]
        return carry
    lax.fori_loop(0, tm // rc, body, 0)

    logits = jnp.dot(h_sc[...], rw_ref[...], preferred_element_type=F32) + rb_ref[...]
    lane = lax.broadcasted_iota(jnp.int32, logits.shape, 1).astype(F32)
    vals, sels, hots = [], [], []
    cur = logits
    for _ in range(TOP_K):
        mx = jnp.max(cur, axis=-1, keepdims=True)
        sel = jnp.min(jnp.where(cur == mx, lane, float(LANES)), axis=-1, keepdims=True)
        hot = lane == sel
        vals.append(mx)
        sels.append(sel)
        hots.append(hot)
        cur = jnp.where(hot, -jnp.inf, cur)
    exps = [jnp.exp(v - vals[0]) for v in vals]
    denom = exps[0] + exps[1] + exps[2] + exps[3]

    hot_sum = jnp.zeros(logits.shape, F32)
    for hot in hots:
        hot_sum = hot_sum + jnp.where(hot, 1.0, 0.0)
    rr = lax.broadcasted_iota(jnp.int32, (tm, tm), 0)
    cc = lax.broadcasted_iota(jnp.int32, (tm, tm), 1)
    below = jnp.where(cc < rr, 1.0, 0.0).astype(BF16)
    base = jnp.dot(below, hot_sum.astype(BF16), preferred_element_type=F32) + cnt_sc[...]

    idx_out = jnp.zeros(logits.shape, F32)
    wts_out = jnp.zeros(logits.shape, F32)
    rank_out = jnp.zeros(logits.shape, F32)
    for k in range(TOP_K):
        col = lane == float(k)
        rank_k = jnp.sum(jnp.where(hots[k], base, 0.0), axis=-1, keepdims=True)
        idx_out = jnp.where(col, sels[k], idx_out)
        wts_out = jnp.where(col, exps[k] / denom, wts_out)
        rank_out = jnp.where(col, rank_k, rank_out)
    idx_ref[...] = idx_out.astype(jnp.int32)
    wts_ref[...] = wts_out
    rank_ref[...] = rank_out.astype(jnp.int32)
    total = cnt_sc[...] + jnp.sum(hot_sum, axis=0, keepdims=True)
    cnt_sc[...] = total
    cnt_ref[...] = total


def _router(x2d, gain, scale, shift, rw, rb, *, seq, t):
    n_tok, d_model = x2d.shape
    tm, rc = t["rt_tm"], t["norm_rc"]
    tpb = seq // tm
    half = d_model // 2
    kern = functools.partial(_router_kernel, tm=tm, rc=rc, half=half)
    return pl.pallas_call(
        kern,
        grid=(n_tok // tm,),
        in_specs=[
            pl.BlockSpec((tm, d_model), lambda i: (i, 0)),
            pl.BlockSpec((1, d_model), lambda i: (0, 0)),
            pl.BlockSpec((1, 1, d_model), lambda i: (i // tpb, 0, 0)),
            pl.BlockSpec((1, 1, d_model), lambda i: (i // tpb, 0, 0)),
            pl.BlockSpec((d_model, LANES), lambda i: (0, 0)),
            pl.BlockSpec((1, LANES), lambda i: (0, 0)),
        ],
        out_specs=[
            pl.BlockSpec((tm * (half // LANES), LANES), lambda i: (i, 0)),
            pl.BlockSpec((tm, LANES), lambda i: (i, 0)),
            pl.BlockSpec((tm, LANES), lambda i: (i, 0)),
            pl.BlockSpec((tm, LANES), lambda i: (i, 0)),
            pl.BlockSpec((1, LANES), lambda i: (0, 0)),
        ],
        out_shape=[
            jax.ShapeDtypeStruct((n_tok * (half // LANES), LANES), jnp.uint32),
            jax.ShapeDtypeStruct((n_tok, LANES), jnp.int32),
            jax.ShapeDtypeStruct((n_tok, LANES), F32),
            jax.ShapeDtypeStruct((n_tok, LANES), jnp.int32),
            jax.ShapeDtypeStruct((1, LANES), F32),
        ],
        scratch_shapes=[pltpu.VMEM((tm, d_model), BF16), pltpu.VMEM((1, LANES), F32)],
        compiler_params=_cparams("arbitrary"),
        name="norm_router_topk",
    )(x2d, gain, scale, shift, rw, rb)


def _expert_kernel(be_ref, na_ref, tok_ref, nxt_ref, src_ref, w1_ref, b1_ref, w2_ref, b2_ref,
                   o_ref, xbuf, sems, *, bm, half, fdim, xp):
    del be_ref
    i = pl.program_id(0)
    n_active = na_ref[0]

    def slab_copy(toks, r, slot):
        src0 = pl.multiple_of(toks[r] * xp, xp)
        dst0 = pl.multiple_of(r * xp, xp)
        return pltpu.make_async_copy(src_ref.at[pl.ds(src0, xp), :],
                                     xbuf.at[slot, pl.ds(dst0, xp), :], sems.at[slot])

    def gather_start(toks, slot):
        def issue(g, carry):
            for u in range(DMA_UNROLL):
                slab_copy(toks, g * DMA_UNROLL + u, slot).start(priority=u % 2)
            return carry
        lax.fori_loop(0, bm // DMA_UNROLL, issue, 0)

    def gather_wait(toks, slot):
        def drain(g, carry):
            for u in range(DMA_UNROLL):
                slab_copy(toks, g * DMA_UNROLL + u, slot).wait()
            return carry
        lax.fori_loop(0, bm // DMA_UNROLL, drain, 0)

    def compute(slot):
        words = jnp.concatenate([xbuf[slot, pl.ds(s, bm, stride=xp), :] for s in range(xp)], axis=1)
        lo = lax.bitcast_convert_type(lax.shift_left(words, jnp.uint32(16)), F32).astype(BF16)
        hi = lax.bitcast_convert_type(words & jnp.uint32(0xFFFF0000), F32).astype(BF16)
        hid = (jnp.dot(lo, w1_ref[0, :half, :], preferred_element_type=F32)
               + jnp.dot(hi, w1_ref[0, half:, :], preferred_element_type=F32)) + b1_ref[0]
        glu = jnp.minimum(hid[:, :fdim], SWIGLU_LIMIT)
        lin = jnp.clip(hid[:, fdim:], -SWIGLU_LIMIT, SWIGLU_LIMIT)
        act = (glu / (1.0 + jnp.exp(-SWIGLU_ALPHA * glu))) * (lin + 1.0)
        y = jnp.dot(act.astype(BF16), w2_ref[0], preferred_element_type=F32) + b2_ref[0]
        bits = lax.bitcast_convert_type(y.astype(BF16).astype(F32), jnp.uint32)
        out_words = lax.shift_right_logical(bits[:, :half], jnp.uint32(16)) | (
            bits[:, half:] & jnp.uint32(0xFFFF0000))
        for c in range(xp):
            o_ref[pl.ds(c, bm, stride=xp), :] = out_words[:, c * LANES:(c + 1) * LANES]

    @pl.when(i == 0)
    def _first():
        gather_start(tok_ref, 0)

    for slot in range(2):
        @pl.when((i % 2 == slot) & (i + 1 < n_active))
        def _prefetch(slot=slot):
            gather_start(nxt_ref, 1 - slot)

        @pl.when((i % 2 == slot) & (i < n_active))
        def _active(slot=slot):
            gather_wait(tok_ref, slot)
            compute(slot)

    @pl.when(i >= n_active)
    def _unused():
        o_ref[...] = jnp.zeros(o_ref.shape, o_ref.dtype)


def _experts(block_e, n_active, row_tok, packed, w1_all, b1_all, w2_all, b2_all, layer, *, bm, xp):
    n_blocks = row_tok.shape[0] // bm
    depth, n_exp, d_model, two_f = w1_all.shape
    fdim, half = two_f // 2, d_model // 2
    grid_spec = pltpu.PrefetchScalarGridSpec(
        num_scalar_prefetch=2,
        grid=(n_blocks,),
        in_specs=[
            pl.BlockSpec((bm,), lambda i, be, na: (i,), memory_space=pltpu.SMEM),
            pl.BlockSpec((bm,), lambda i, be, na: (jnp.minimum(i + 1, n_blocks - 1),),
                         memory_space=pltpu.SMEM),
            pl.BlockSpec(memory_space=pl.ANY),
            pl.BlockSpec((None, 1, d_model, two_f), lambda i, be, na: (layer, be[i], 0, 0)),
            pl.BlockSpec((None, 1, 1, two_f), lambda i, be, na: (layer, be[i], 0, 0)),
            pl.BlockSpec((None, 1, fdim, d_model), lambda i, be, na: (layer, be[i], 0, 0)),
            pl.BlockSpec((None, 1, 1, d_model), lambda i, be, na: (layer, be[i], 0, 0)),
        ],
        out_specs=pl.BlockSpec((bm * xp, LANES), lambda i, be, na: (i, 0)),
        scratch_shapes=[pltpu.VMEM((2, bm * xp, LANES), packed.dtype),
                        pltpu.SemaphoreType.DMA((2,))],
    )
    return pl.pallas_call(
        functools.partial(_expert_kernel, bm=bm, half=half, fdim=fdim, xp=xp),
        grid_spec=grid_spec,
        out_shape=jax.ShapeDtypeStruct((n_blocks * bm * xp, LANES), packed.dtype),
        compiler_params=_cparams("arbitrary"),
        name="moe_experts",
    )(block_e, n_active, row_tok, row_tok, packed, w1_all, b1_all.reshape(depth, n_exp, 1, two_f),
      w2_all, b2_all.reshape(depth, n_exp, 1, d_model))


def _combine_kernel(dest_ref, nxt_ref, y_ref, wts_ref, x_ref, g_ref, o_ref, buf, sems,
                    *, tm, yp, half, n_steps):
    i = pl.program_id(0)

    def slab_copy(dests, r, k, slot):
        src0 = pl.multiple_of(dests[r * TOP_K + k] * yp, yp)
        dst0 = pl.multiple_of(r * yp, yp)
        return pltpu.make_async_copy(y_ref.at[pl.ds(src0, yp), :],
                                     buf.at[slot, k, pl.ds(dst0, yp), :], sems.at[slot])

    def gather_start(dests, slot):
        def issue(g, carry):
            for u in range(DMA_UNROLL // TOP_K):
                for k in range(TOP_K):
                    slab_copy(dests, g * (DMA_UNROLL // TOP_K) + u, k, slot).start(priority=k % 2)
            return carry
        lax.fori_loop(0, tm * TOP_K // DMA_UNROLL, issue, 0)

    def gather_wait(dests, slot):
        def drain(g, carry):
            for u in range(DMA_UNROLL // TOP_K):
                for k in range(TOP_K):
                    slab_copy(dests, g * (DMA_UNROLL // TOP_K) + u, k, slot).wait()
            return carry
        lax.fori_loop(0, tm * TOP_K // DMA_UNROLL, drain, 0)

    def compute(slot):
        w = wts_ref[...]
        wk = [jnp.broadcast_to(w[:, k:k + 1], (tm, LANES)) for k in range(TOP_K)]
        for c in range(yp):
            acc_lo = acc_hi = None
            for k in range(TOP_K):
                words = buf[slot, k, pl.ds(c, tm, stride=yp), :]
                lo = wk[k] * lax.bitcast_convert_type(lax.shift_left(words, jnp.uint32(16)), F32)
                hi = wk[k] * lax.bitcast_convert_type(words & jnp.uint32(0xFFFF0000), F32)
                acc_lo = lo if acc_lo is None else acc_lo + lo
                acc_hi = hi if acc_hi is None else acc_hi + hi
            for base, acc in ((0, acc_lo), (half, acc_hi)):
                sl = slice(base + c * LANES, base + (c + 1) * LANES)
                o_ref[:, sl] = x_ref[:, sl] + g_ref[0, :, sl] * acc

    @pl.when(i == 0)
    def _first():
        gather_start(dest_ref, 0)

    for slot in range(2):
        @pl.when((i % 2 == slot) & (i + 1 < n_steps))
        def _prefetch(slot=slot):
            gather_start(nxt_ref, 1 - slot)

        @pl.when(i % 2 == slot)
        def _active(slot=slot):
            gather_wait(dest_ref, slot)
            compute(slot)


def _combine(dest, ys, yp, wts, x2d, gate, *, seq, t):
    n_tok, d_model = x2d.shape
    tm = t["cmb_tm"]
    tpb = seq // tm
    n_steps = n_tok // tm
    return pl.pallas_call(
        functools.partial(_combine_kernel, tm=tm, yp=yp, half=d_model // 2, n_steps=n_steps),
        grid=(n_steps,),
        in_specs=[
            pl.BlockSpec((tm * TOP_K,), lambda i: (i,), memory_space=pltpu.SMEM),
            pl.BlockSpec((tm * TOP_K,), lambda i: (jnp.minimum(i + 1, n_steps - 1),),
                         memory_space=pltpu.SMEM),
            pl.BlockSpec(memory_space=pl.ANY),
            pl.BlockSpec((tm, LANES), lambda i: (i, 0)),
            pl.BlockSpec((tm, d_model), lambda i: (i, 0)),
            pl.BlockSpec((1, 1, d_model), lambda i: (i // tpb, 0, 0)),
        ],
        out_specs=pl.BlockSpec((tm, d_model), lambda i: (i, 0)),
        out_shape=jax.ShapeDtypeStruct((n_tok, d_model), F32),
        scratch_shapes=[pltpu.VMEM((2, TOP_K, tm * yp, LANES), ys.dtype), pltpu.SemaphoreType.DMA((2,))],
        compiler_params=_cparams("arbitrary"),
        name="moe_combine_residual",
    )(dest, dest, ys, wts, x2d, gate)


def _layer(layer, x2d, mod, norm1_g, w_in_f, w_in_all, forget_b, pool_w, pool_scale, q_norm_g, k_norm_g,
           w_out_all, norm2_g, router_w, router_b, w1_all, b1_all, w2_all, b2_all, *, seq, t):
    n_tok, d_model = x2d.shape
    batch = n_tok // seq
    n_groups, gd, _ = pool_w.shape
    pool_width = n_groups * gd
    att_width = d_model - pool_width
    n_heads = att_width // HEAD_DIM
    n_exp = router_w.shape[1]
    assert pool_width == att_width and n_groups == len(POOL_WINDOWS) and gd % HEAD_DIM == 0
    n_main = pool_width + 3 * att_width
    hp = -(-n_heads // BF16_ROWS) * BF16_ROWS

    shift1, scale1, gate1, shift2, scale2, gate2 = [
        m.reshape(batch, 1, d_model) for m in jnp.split(mod, N_MOD, axis=-1)]

    wf_src = lax.optimization_barrier(w_in_f[:, n_main:])
    wf_t = jnp.zeros((hp, d_model), BF16).at[:n_heads].set(wf_src.T.astype(BF16))
    fb = jnp.zeros((hp, 1), F32).at[:n_heads, 0].set(forget_b)
    reps = gd // HEAD_DIM
    h1 = _norm_modulate(x2d, norm1_g.reshape(1, d_model), scale1, shift1, seq=seq, t=t)
    main, cum = _in_projection(
        h1, w_in_all, layer, n_main, wf_t, fb,
        pool_w.astype(BF16), pool_scale.reshape(n_groups, 1, gd),
        jnp.tile(q_norm_g, reps).reshape(1, gd), jnp.tile(k_norm_g, reps).reshape(1, gd),
        seq=seq, t=t)
    att = _attention(main, cum, seq=seq, n_heads=n_heads, q_off=pool_width,
                     k_off=pool_width + att_width, v_off=pool_width + 2 * att_width, t=t)
    x1 = _out_projection(main, att, w_out_all, layer, x2d, gate1, seq=seq,
                         pool_width=pool_width, t=t)

    rw = jnp.zeros((d_model, LANES), BF16).at[:, :n_exp].set(router_w.astype(BF16))
    rb = jnp.full((1, LANES), -jnp.inf, F32).at[0, :n_exp].set(router_b)
    packed, idx, wts, rank, counts = _router(
        x1, norm2_g.reshape(1, d_model), scale2, shift2, rw, rb, seq=seq, t=t)

    bm = t["moe_bm"]
    n_assign = n_tok * TOP_K
    n_blocks = -(-n_assign // bm) + n_exp
    counts = counts[0, :n_exp].astype(jnp.int32)
    padded = (counts + bm - 1) // bm * bm
    pad_end = jnp.cumsum(padded)
    pad_start = pad_end - padded
    dest = (pad_start[idx[:, :TOP_K]] + rank[:, :TOP_K]).reshape(-1)
    row_tok = jnp.zeros((n_blocks * bm,), jnp.int32).at[dest].set(
        jnp.arange(n_assign, dtype=jnp.int32) // TOP_K,
        unique_indices=True, mode="promise_in_bounds")
    block_start = jnp.arange(n_blocks, dtype=jnp.int32) * bm
    block_e = jnp.minimum(jnp.sum(block_start[:, None] >= pad_end[None, :], axis=1),
                          n_exp - 1).astype(jnp.int32)
    n_active = (pad_end[-1:] // bm).astype(jnp.int32)

    xp = d_model // 2 // LANES
    ys = _experts(block_e, n_active, row_tok, packed, w1_all, b1_all, w2_all, b2_all, layer,
                  bm=bm, xp=xp)
    return _combine(dest, ys, xp, wts, x1, gate2, seq=seq, t=t)


def kernel(x, c, ada_w, ada_b, norm1_g, w_in, forget_b, pool_w, pool_scale, q_norm_g, k_norm_g,
           w_out, norm2_g, router_w, router_b, expert_w1, expert_b1, expert_w2, expert_b2):
    batch, seq, d_model = x.shape
    depth = ada_w.shape[0]
    t = _tiles(batch * seq, seq, d_model)
    mod = _modulation(c, ada_w, ada_b, t["mod_tn"])
    x2d = x.reshape(batch * seq, d_model)
    w_in_all, w_out_all = w_in.astype(BF16), w_out.astype(BF16)
    w1_all, w2_all = expert_w1.astype(BF16), expert_w2.astype(BF16)
    for l in range(depth):
        x2d = _layer(l, x2d, mod[l], norm1_g[l], w_in[l], w_in_all, forget_b[l], pool_w[l], pool_scale[l],
                     q_norm_g[l], k_norm_g[l], w_out_all, norm2_g[l], router_w[l], router_b[l],
                     w1_all, expert_b1, w2_all, expert_b2, seq=seq, t=t)
    return x2d.reshape(batch, seq, d_model)
```

```python
import functools

import jax
import jax.numpy as jnp
from jax import lax
from jax.experimental import pallas as pl
from jax.experimental.pallas import tpu as pltpu

HEAD_DIM = 128
POOL_WINDOWS = (2, 4, 8, 16)
TOP_K = 4
N_MOD = 6
NORM_EPS = 1e-6
SWIGLU_ALPHA = 1.702
SWIGLU_LIMIT = 7.0
LOG2_E = 1.4426950408889634
ATT_LOGIT_SCALE = HEAD_DIM ** -0.5 * LOG2_E

LANES = 128
SUBLANES = 8
BF16_ROWS = 16
DMA_UNROLL = 8
POOL_HALO = 16
VMEM_LIMIT_BYTES = 56 * 2**20

F32 = jnp.float32
BF16 = jnp.bfloat16


def _cparams(*sem):
    return pltpu.CompilerParams(dimension_semantics=sem, vmem_limit_bytes=VMEM_LIMIT_BYTES)


def _tiles(n_tok, seq, d_model):
    return dict(
        mod_tn=min(512, d_model),
        norm_tm=min(512, seq),
        in_tm=min(1024, seq),
        norm_rc=min(64, seq),
        att_tq=min(1024, seq),
        att_tk=512,
        att_heads=2,
        out_tm=min(1024, seq),
        out_tn=min(512, d_model),
        rt_tm=min(512, seq),
        moe_bm=min(256, n_tok),
        cmb_tm=min(128, seq),
    )


def _mod_kernel(c_ref, w_ref, b_ref, o_ref):
    c = c_ref[...]
    ca = c / (1.0 + jnp.exp(-c))
    w = w_ref[0].astype(BF16)
    o_ref[0] = jnp.dot(ca.astype(BF16), w, preferred_element_type=F32) + b_ref[0]


def _modulation(c, ada_w, ada_b, tn):
    depth, d_model, n_out = ada_w.shape
    b = c.shape[0]
    c_pad = jnp.zeros((SUBLANES, d_model), F32).at[:b].set(c)
    out = pl.pallas_call(
        _mod_kernel,
        grid=(depth, n_out // tn),
        in_specs=[
            pl.BlockSpec((SUBLANES, d_model), lambda l, j: (0, 0)),
            pl.BlockSpec((1, d_model, tn), lambda l, j: (l, 0, j)),
            pl.BlockSpec((1, 1, tn), lambda l, j: (l, 0, j)),
        ],
        out_specs=pl.BlockSpec((1, SUBLANES, tn), lambda l, j: (l, 0, j)),
        out_shape=jax.ShapeDtypeStruct((depth, SUBLANES, n_out), F32),
        compiler_params=_cparams("arbitrary", "arbitrary"),
        name="adaln_mod",
    )(c_pad, ada_w, ada_b.reshape(depth, 1, n_out))
    return out[:, :b]


def _norm_kernel(x_ref, g_ref, sc_ref, sh_ref, h_ref, *, tm, rc):
    def body(c, carry):
        r0 = pl.multiple_of(c * rc, rc)
        x = x_ref[pl.ds(r0, rc), :]
        ms = jnp.mean(x * x, axis=-1, keepdims=True)
        y = x * lax.rsqrt(ms + NORM_EPS)
        h = (y * g_ref[...]) * (1.0 + sc_ref[0]) + sh_ref[0]
        h_ref[pl.ds(r0, rc), :] = h.astype(BF16)
        return carry
    lax.fori_loop(0, tm // rc, body, 0)


def _norm_modulate(x2d, gain, scale, shift, *, seq, t):
    n_tok, d_model = x2d.shape
    tm, rc = t["norm_tm"], t["norm_rc"]
    tpb = seq // tm
    return pl.pallas_call(
        functools.partial(_norm_kernel, tm=tm, rc=rc),
        grid=(n_tok // tm,),
        in_specs=[
            pl.BlockSpec((tm, d_model), lambda i: (i, 0)),
            pl.BlockSpec((1, d_model), lambda i: (0, 0)),
            pl.BlockSpec((1, 1, d_model), lambda i: (i // tpb, 0, 0)),
            pl.BlockSpec((1, 1, d_model), lambda i: (i // tpb, 0, 0)),
        ],
        out_specs=pl.BlockSpec((tm, d_model), lambda i: (i, 0)),
        out_shape=jax.ShapeDtypeStruct((n_tok, d_model), BF16),
        compiler_params=_cparams("arbitrary"),
        name="norm_modulate",
    )(x2d, gain, scale, shift)


def _in_kernel(h_ref, w_ref, wf_ref, fb_ref, pw_ref, ps_ref, qg_ref, kg_ref,
               main_ref, cum_ref, tail_sc, fc_sc, *, tm, tpb, gd):
    i = pl.program_id(0)
    j = pl.program_id(1)
    first = (i % tpb) == 0

    def proj():
        return jnp.dot(h_ref[...], w_ref[...], preferred_element_type=F32)

    for g, win in enumerate(POOL_WINDOWS):
        @pl.when(j == g)
        def _pool(g=g, win=win):
            u = proj()
            prev = jnp.where(first, 0.0, tail_sc[g])
            s = jnp.concatenate([prev, u], axis=0)
            shift = 1
            while shift < win:
                s = s + pltpu.roll(s, shift, 0)
                shift *= 2
            wsum = s[POOL_HALO:]
            pos = (i % tpb) * tm + lax.broadcasted_iota(jnp.int32, (tm, 1), 0)
            cnt = jnp.minimum(pos + 1, win).astype(F32)
            mixed = wsum / cnt - u
            po = jnp.dot(mixed.astype(BF16), pw_ref[0], preferred_element_type=F32) * ps_ref[0]
            main_ref[...] = po.astype(BF16)
            tail_sc[g] = u[tm - POOL_HALO:]

    def qk_norm(gain_ref, post_scale):
        r = proj()
        for c in range(gd // HEAD_DIM):
            sl = slice(c * HEAD_DIM, (c + 1) * HEAD_DIM)
            rc_ = r[:, sl]
            ms = jnp.mean(rc_ * rc_, axis=-1, keepdims=True)
            normed = (rc_ * lax.rsqrt(ms + NORM_EPS)) * gain_ref[:, sl]
            if post_scale is not None:
                normed = normed * post_scale
            main_ref[:, sl] = normed.astype(BF16)

    @pl.when((j >= 4) & (j < 8))
    def _q():
        qk_norm(qg_ref, ATT_LOGIT_SCALE)

    @pl.when((j >= 8) & (j < 12))
    def _k():
        qk_norm(kg_ref, None)

    @pl.when((j >= 12) & (j < 16))
    def _v():
        main_ref[...] = proj().astype(BF16)

    @pl.when(j == 16)
    def _forget():
        z = lax.dot_general(wf_ref[...], h_ref[...], (((1,), (1,)), ((), ())),
                            preferred_element_type=F32) + fb_ref[...]
        ls = jnp.minimum(z, 0.0) - jnp.log1p(jnp.exp(-jnp.abs(z)))
        p0 = ls.astype(BF16)
        r1 = ls - p0.astype(F32)
        p1 = r1.astype(BF16)
        p2 = (r1 - p1.astype(F32)).astype(BF16)
        rr = lax.broadcasted_iota(jnp.int32, (tm, tm), 0)
        cc = lax.broadcasted_iota(jnp.int32, (tm, tm), 1)
        tri = jnp.where(rr <= cc, 1.0, 0.0).astype(BF16)
        cum = (jnp.dot(p0, tri, preferred_element_type=F32)
               + jnp.dot(p1, tri, preferred_element_type=F32)
               + jnp.dot(p2, tri, preferred_element_type=F32))
        cum = cum + jnp.where(first, 0.0, fc_sc[:, 0:1])
        cum_ref[0] = cum * LOG2_E
        fc_sc[...] = jnp.broadcast_to(cum[:, tm - 1:tm], fc_sc.shape)


def _in_projection(h2d, w_in_all, layer, n_main, wf_t, fb, pool_w, pool_scale, qg, kg, *, seq, t):
    n_tok, d_model = h2d.shape
    batch = n_tok // seq
    tm = t["in_tm"]
    gd = pool_w.shape[-1]
    hp = wf_t.shape[0]
    tpb = seq // tm
    nj = n_main // gd + 1
    last = n_main // gd - 1
    kern = functools.partial(_in_kernel, tm=tm, tpb=tpb, gd=gd)
    return pl.pallas_call(
        kern,
        grid=(n_tok // tm, nj),
        in_specs=[
            pl.BlockSpec((tm, d_model), lambda i, j: (i, 0)),
            pl.BlockSpec((None, d_model, gd), lambda i, j: (layer, 0, jnp.minimum(j, last))),
            pl.BlockSpec((hp, d_model), lambda i, j: (0, 0)),
            pl.BlockSpec((hp, 1), lambda i, j: (0, 0)),
            pl.BlockSpec((1, gd, gd), lambda i, j: (jnp.minimum(j, 3), 0, 0)),
            pl.BlockSpec((1, 1, gd), lambda i, j: (jnp.minimum(j, 3), 0, 0)),
            pl.BlockSpec((1, gd), lambda i, j: (0, 0)),
            pl.BlockSpec((1, gd), lambda i, j: (0, 0)),
        ],
        out_specs=[
            pl.BlockSpec((tm, gd), lambda i, j: (i, jnp.minimum(j, last))),
            pl.BlockSpec((1, hp, tm), lambda i, j: (i // tpb, 0, i % tpb)),
        ],
        out_shape=[
            jax.ShapeDtypeStruct((n_tok, n_main), BF16),
            jax.ShapeDtypeStruct((batch, hp, seq), F32),
        ],
        scratch_shapes=[
            pltpu.VMEM((len(POOL_WINDOWS), POOL_HALO, gd), F32),
            pltpu.VMEM((hp, LANES), F32),
        ],
        compiler_params=_cparams("arbitrary", "arbitrary"),
        name="in_proj",
    )(h2d, w_in_all, wf_t, fb, pool_w, pool_scale, qg, kg)


def _attn_kernel(q_ref, k_ref, v_ref, c_ref, w1_ref, w2_ref, o_ref, w1o_ref, w2o_ref, *, tq, tk, hps):
    w1o_ref[...] = w1_ref[...].astype(BF16)
    w2o_ref[...] = w2_ref[...].astype(BF16)
    qi = pl.program_id(2)
    heads = [slice(h * HEAD_DIM, (h + 1) * HEAD_DIM) for h in range(hps)]
    kpq = tq // tk

    def block(h, kj, carry, diag_index):
        m_prev, l_prev, acc_prev = carry
        start = pl.multiple_of(kj * tk, tk)
        k = k_ref[pl.ds(start, tk), heads[h]]
        v = v_ref[pl.ds(start, tk), heads[h]]
        s = lax.dot_general(q_ref[:, heads[h]], k, (((1,), (1,)), ((), ())),
                            preferred_element_type=F32)
        s = s - c_ref[h, :, pl.ds(start, tk)]
        if diag_index is not None:
            rr = lax.broadcasted_iota(jnp.int32, (tq, tk), 0)
            cc = lax.broadcasted_iota(jnp.int32, (tq, tk), 1)
            s = jnp.where(cc + diag_index * tk <= rr, s, -jnp.inf)
        m_new = jnp.maximum(m_prev, jnp.max(s, axis=-1, keepdims=True))
        p = jnp.exp2(s - m_new)
        alpha = jnp.exp2(m_prev - m_new)
        l_new = alpha * l_prev + jnp.sum(p, axis=-1, keepdims=True)
        acc_new = alpha * acc_prev + jnp.dot(p.astype(BF16), v, preferred_element_type=F32)
        return m_new, l_new, acc_new

    def full_blocks(g, carry):
        for d in range(kpq):
            carry = tuple(block(h, g * kpq + d, carry[h], None) for h in range(hps))
        return carry

    def diagonal_blocks(_, carry):
        for d in range(kpq):
            carry = tuple(block(h, qi * kpq + d, carry[h], d) for h in range(hps))
        return carry

    init = tuple((jnp.full((tq, 1), -jnp.inf, F32), jnp.zeros((tq, 1), F32),
                  jnp.zeros((tq, HEAD_DIM), F32)) for _ in range(hps))
    carry = lax.fori_loop(0, qi, full_blocks, init)
    carry = lax.fori_loop(0, jnp.minimum(qi + 1, 1), diagonal_blocks, carry)
    for h in range(hps):
        _, l_f, acc_f = carry[h]
        o_ref[:, heads[h]] = (acc_f / l_f).astype(o_ref.dtype)


def _attention(main, cum, w1_f32, w2_f32, layer, *, seq, n_heads, q_off, k_off, v_off, t):
    n_tok = main.shape[0]
    batch = n_tok // seq
    tq, hps = t["att_tq"], min(t["att_heads"], n_heads)
    nq = seq // tq
    hp = cum.shape[1]
    width = hps * HEAD_DIM
    assert n_heads % hps == 0 and hp % hps == 0
    cum3 = cum.reshape(batch * hp, 1, seq)
    kern = functools.partial(_attn_kernel, tq=tq, tk=min(t["att_tk"], tq), hps=hps)
    qb, kb, vb, cb = q_off // width, k_off // width, v_off // width, hp // hps
    n_hg = n_heads // hps
    n_steps = batch * n_hg * nq
    depth, n_exp, d_model, two_f = w1_f32.shape
    fdim = w2_f32.shape[2]
    rows1, rows2 = n_exp * d_model, n_exp * fdim
    assert rows1 % (n_steps * BF16_ROWS) == 0 and rows2 % (n_steps * BF16_ROWS) == 0
    r1, r2 = rows1 // n_steps, rows2 // n_steps

    def step(b, h, i):
        return (b * n_hg + h) * nq + i

    att, w1b, w2b = pl.pallas_call(
        kern,
        grid=(batch, n_hg, nq),
        in_specs=[
            pl.BlockSpec((tq, width), lambda b, h, i: (b * nq + i, qb + h)),
            pl.BlockSpec((seq, width), lambda b, h, i: (b, kb + h)),
            pl.BlockSpec((seq, width), lambda b, h, i: (b, vb + h)),
            pl.BlockSpec((hps, 1, seq), lambda b, h, i: (b * cb + h, 0, 0)),
            pl.BlockSpec((None, r1, two_f), lambda b, h, i: (layer, step(b, h, i), 0)),
            pl.BlockSpec((None, r2, d_model), lambda b, h, i: (layer, step(b, h, i), 0)),
        ],
        out_specs=[
            pl.BlockSpec((tq, width), lambda b, h, i: (b * nq + i, h)),
            pl.BlockSpec((r1, two_f), lambda b, h, i: (step(b, h, i), 0)),
            pl.BlockSpec((r2, d_model), lambda b, h, i: (step(b, h, i), 0)),
        ],
        out_shape=[
            jax.ShapeDtypeStruct((n_tok, n_heads * HEAD_DIM), BF16),
            jax.ShapeDtypeStruct((rows1, two_f), BF16),
            jax.ShapeDtypeStruct((rows2, d_model), BF16),
        ],
        compiler_params=_cparams("arbitrary", "arbitrary", "arbitrary"),
        name="forget_attention",
    )(main, main, main, cum3, w1_f32.reshape(depth, rows1, two_f), w2_f32.reshape(depth, rows2, d_model))
    return att, w1b.reshape(n_exp, d_model, two_f), w2b.reshape(n_exp, fdim, d_model)


def _out_kernel(po_ref, at_ref, wp_ref, wa_ref, x_ref, g_ref, o_ref):
    mix = (jnp.dot(po_ref[...], wp_ref[...], preferred_element_type=F32)
           + jnp.dot(at_ref[...], wa_ref[...], preferred_element_type=F32))
    o_ref[...] = x_ref[...] + g_ref[0] * mix


def _out_projection(main, att, w_out_all, layer, x2d, gate, *, seq, pool_width, t):
    n_tok, d_model = x2d.shape
    tm, tn = t["out_tm"], t["out_tn"]
    att_width = att.shape[1]
    tpb = seq // tm
    return pl.pallas_call(
        _out_kernel,
        grid=(n_tok // tm, d_model // tn),
        in_specs=[
            pl.BlockSpec((tm, pool_width), lambda i, j: (i, 0)),
            pl.BlockSpec((tm, att_width), lambda i, j: (i, 0)),
            pl.BlockSpec((None, pool_width, tn), lambda i, j: (layer, 0, j)),
            pl.BlockSpec((None, att_width, tn), lambda i, j: (layer, pool_width // att_width, j)),
            pl.BlockSpec((tm, tn), lambda i, j: (i, j)),
            pl.BlockSpec((1, 1, tn), lambda i, j: (i // tpb, 0, j)),
        ],
        out_specs=pl.BlockSpec((tm, tn), lambda i, j: (i, j)),
        out_shape=jax.ShapeDtypeStruct((n_tok, d_model), F32),
        compiler_params=_cparams("arbitrary", "arbitrary"),
        name="out_proj_residual",
    )(main, att, w_out_all, w_out_all, x2d, gate)


def _router_kernel(x_ref, g_ref, sc_ref, sh_ref, rw_ref, rb_ref,
                   hp_ref, idx_ref, wts_ref, rank_ref, cnt_ref, h_sc, cnt_sc, *, tm, rc, half):
    i = pl.program_id(0)

    @pl.when(i == 0)
    def _init():
        cnt_sc[...] = jnp.zeros(cnt_sc.shape, F32)

    def body(c, carry):
        r0 = pl.multiple_of(c * rc, rc)
        x = x_ref[pl.ds(r0, rc), :]
        ms = jnp.mean(x * x, axis=-1, keepdims=True)
        y = x * lax.rsqrt(ms + NORM_EPS)
        h = ((y * g_ref[...]) * (1.0 + sc_ref[0]) + sh_ref[0]).astype(BF16)
        h_sc[pl.ds(r0, rc), :] = h
        bits = lax.bitcast_convert_type(h.astype(F32), jnp.uint32)
        lo = lax.shift_right_logical(bits[:, :half], jnp.uint32(16))
        hi = bits[:, half:] & jnp.uint32(0xFFFF0000)
        words = lo | hi
        pitch = half // LANES
        for s in range(pitch):
            hp_ref[pl.ds(r0 * pitch + s, rc, stride=pitch), :] = words[:, s * LANES:(s + 1) * LANES]
        return carry
    lax.fori_loop(0, tm // rc, body, 0)

    logits = jnp.dot(h_sc[...], rw_ref[...], preferred_element_type=F32) + rb_ref[...]
    lane = lax.broadcasted_iota(jnp.int32, logits.shape, 1).astype(F32)
    vals, sels, hots = [], [], []
    cur = logits
    for _ in range(TOP_K):
        mx = jnp.max(cur, axis=-1, keepdims=True)
        sel = jnp.min(jnp.where(cur == mx, lane, float(LANES)), axis=-1, keepdims=True)
        hot = lane == sel
        vals.append(mx)
        sels.append(sel)
        hots.append(hot)
        cur = jnp.where(hot, -jnp.inf, cur)
    exps = [jnp.exp(v - vals[0]) for v in vals]
    denom = exps[0] + exps[1] + exps[2] + exps[3]

    hot_sum = jnp.zeros(logits.shape, F32)
    for hot in hots:
        hot_sum = hot_sum + jnp.where(hot, 1.0, 0.0)
    rr = lax.broadcasted_iota(jnp.int32, (tm, tm), 0)
    cc = lax.broadcasted_iota(jnp.int32, (tm, tm), 1)
    below = jnp.where(cc < rr, 1.0, 0.0).astype(BF16)
    base = jnp.dot(below, hot_sum.astype(BF16), preferred_element_type=F32) + cnt_sc[...]

    idx_out = jnp.zeros(logits.shape, F32)
    wts_out = jnp.zeros(logits.shape, F32)
    rank_out = jnp.zeros(logits.shape, F32)
    for k in range(TOP_K):
        col = lane == float(k)
        rank_k = jnp.sum(jnp.where(hots[k], base, 0.0), axis=-1, keepdims=True)
        idx_out = jnp.where(col, sels[k], idx_out)
        wts_out = jnp.where(col, exps[k] / denom, wts_out)
        rank_out = jnp.where(col, rank_k, rank_out)
    idx_ref[...] = idx_out.astype(jnp.int32)
    wts_ref[...] = wts_out
    rank_ref[...] = rank_out.astype(jnp.int32)
    total = cnt_sc[...] + jnp.sum(hot_sum, axis=0, keepdims=True)
    cnt_sc[...] = total
    cnt_ref[...] = total


def _router(x2d, gain, scale, shift, rw, rb, *, seq, t):
    n_tok, d_model = x2d.shape
    tm, rc = t["rt_tm"], t["norm_rc"]
    tpb = seq // tm
    half = d_model // 2
    kern = functools.partial(_router_kernel, tm=tm, rc=rc, half=half)
    return pl.pallas_call(
        kern,
        grid=(n_tok // tm,),
        in_specs=[
            pl.BlockSpec((tm, d_model), lambda i: (i, 0)),
            pl.BlockSpec((1, d_model), lambda i: (0, 0)),
            pl.BlockSpec((1, 1, d_model), lambda i: (i // tpb, 0, 0)),
            pl.BlockSpec((1, 1, d_model), lambda i: (i // tpb, 0, 0)),
            pl.BlockSpec((d_model, LANES), lambda i: (0, 0)),
            pl.BlockSpec((1, LANES), lambda i: (0, 0)),
        ],
        out_specs=[
            pl.BlockSpec((tm * (half // LANES), LANES), lambda i: (i, 0)),
            pl.BlockSpec((tm, LANES), lambda i: (i, 0)),
            pl.BlockSpec((tm, LANES), lambda i: (i, 0)),
            pl.BlockSpec((tm, LANES), lambda i: (i, 0)),
            pl.BlockSpec((1, LANES), lambda i: (0, 0)),
        ],
        out_shape=[
            jax.ShapeDtypeStruct((n_tok * (half // LANES), LANES), jnp.uint32),
            jax.ShapeDtypeStruct((n_tok, LANES), jnp.int32),
            jax.ShapeDtypeStruct((n_tok, LANES), F32),
            jax.ShapeDtypeStruct((n_tok, LANES), jnp.int32),
            jax.ShapeDtypeStruct((1, LANES), F32),
        ],
        scratch_shapes=[pltpu.VMEM((tm, d_model), BF16), pltpu.VMEM((1, LANES), F32)],
        compiler_params=_cparams("arbitrary"),
        name="norm_router_topk",
    )(x2d, gain, scale, shift, rw, rb)


def _expert_kernel(be_ref, na_ref, tok_ref, nxt_ref, src_ref, w1_ref, b1_ref, w2_ref, b2_ref,
                   o_ref, xbuf, sems, *, bm, half, fdim, xp):
    del be_ref
    i = pl.program_id(0)
    n_active = na_ref[0]

    def slab_copy(toks, r, slot):
        src0 = pl.multiple_of(toks[r] * xp, xp)
        dst0 = pl.multiple_of(r * xp, xp)
        return pltpu.make_async_copy(src_ref.at[pl.ds(src0, xp), :],
                                     xbuf.at[slot, pl.ds(dst0, xp), :], sems.at[slot])

    def gather_start(toks, slot):
        def issue(g, carry):
            for u in range(DMA_UNROLL):
                slab_copy(toks, g * DMA_UNROLL + u, slot).start(priority=u % 2)
            return carry
        lax.fori_loop(0, bm // DMA_UNROLL, issue, 0)

    def gather_wait(toks, slot):
        def drain(g, carry):
            for u in range(DMA_UNROLL):
                slab_copy(toks, g * DMA_UNROLL + u, slot).wait()
            return carry
        lax.fori_loop(0, bm // DMA_UNROLL, drain, 0)

    def compute(slot):
        words = jnp.concatenate([xbuf[slot, pl.ds(s, bm, stride=xp), :] for s in range(xp)], axis=1)
        lo = lax.bitcast_convert_type(lax.shift_left(words, jnp.uint32(16)), F32).astype(BF16)
        hi = lax.bitcast_convert_type(words & jnp.uint32(0xFFFF0000), F32).astype(BF16)
        hid = (jnp.dot(lo, w1_ref[0, :half, :], preferred_element_type=F32)
               + jnp.dot(hi, w1_ref[0, half:, :], preferred_element_type=F32)) + b1_ref[0]
        glu = jnp.minimum(hid[:, :fdim], SWIGLU_LIMIT)
        lin = jnp.clip(hid[:, fdim:], -SWIGLU_LIMIT, SWIGLU_LIMIT)
        act = (glu / (1.0 + jnp.exp(-SWIGLU_ALPHA * glu))) * (lin + 1.0)
        y = jnp.dot(act.astype(BF16), w2_ref[0], preferred_element_type=F32) + b2_ref[0]
        bits = lax.bitcast_convert_type(y.astype(BF16).astype(F32), jnp.uint32)
        out_words = lax.shift_right_logical(bits[:, :half], jnp.uint32(16)) | (
            bits[:, half:] & jnp.uint32(0xFFFF0000))
        for c in range(xp):
            o_ref[pl.ds(c, bm, stride=xp), :] = out_words[:, c * LANES:(c + 1) * LANES]

    @pl.when(i == 0)
    def _first():
        gather_start(tok_ref, 0)

    for slot in range(2):
        @pl.when((i % 2 == slot) & (i + 1 < n_active))
        def _prefetch(slot=slot):
            gather_start(nxt_ref, 1 - slot)

        @pl.when((i % 2 == slot) & (i < n_active))
        def _active(slot=slot):
            gather_wait(tok_ref, slot)
            compute(slot)

    @pl.when(i >= n_active)
    def _unused():
        o_ref[...] = jnp.zeros(o_ref.shape, o_ref.dtype)


def _experts(block_e, n_active, row_tok, packed, w1, b1, w2, b2, *, bm, xp):
    n_blocks = row_tok.shape[0] // bm
    n_exp, d_model, two_f = w1.shape
    fdim, half = two_f // 2, d_model // 2
    grid_spec = pltpu.PrefetchScalarGridSpec(
        num_scalar_prefetch=2,
        grid=(n_blocks,),
        in_specs=[
            pl.BlockSpec((bm,), lambda i, be, na: (i,), memory_space=pltpu.SMEM),
            pl.BlockSpec((bm,), lambda i, be, na: (jnp.minimum(i + 1, n_blocks - 1),),
                         memory_space=pltpu.SMEM),
            pl.BlockSpec(memory_space=pl.ANY),
            pl.BlockSpec((1, d_model, two_f), lambda i, be, na: (be[i], 0, 0)),
            pl.BlockSpec((1, 1, two_f), lambda i, be, na: (be[i], 0, 0)),
            pl.BlockSpec((1, fdim, d_model), lambda i, be, na: (be[i], 0, 0)),
            pl.BlockSpec((1, 1, d_model), lambda i, be, na: (be[i], 0, 0)),
        ],
        out_specs=pl.BlockSpec((bm * xp, LANES), lambda i, be, na: (i, 0)),
        scratch_shapes=[pltpu.VMEM((2, bm * xp, LANES), packed.dtype),
                        pltpu.SemaphoreType.DMA((2,))],
    )
    return pl.pallas_call(
        functools.partial(_expert_kernel, bm=bm, half=half, fdim=fdim, xp=xp),
        grid_spec=grid_spec,
        out_shape=jax.ShapeDtypeStruct((n_blocks * bm * xp, LANES), packed.dtype),
        compiler_params=_cparams("arbitrary"),
        name="moe_experts",
    )(block_e, n_active, row_tok, row_tok, packed, w1, b1.reshape(n_exp, 1, two_f),
      w2, b2.reshape(n_exp, 1, d_model))


def _combine_kernel(dest_ref, nxt_ref, y_ref, wts_ref, x_ref, g_ref, o_ref, buf, sems,
                    *, tm, yp, half, n_steps):
    i = pl.program_id(0)

    def slab_copy(dests, r, k, slot):
        src0 = pl.multiple_of(dests[r * TOP_K + k] * yp, yp)
        dst0 = pl.multiple_of(r * yp, yp)
        return pltpu.make_async_copy(y_ref.at[pl.ds(src0, yp), :],
                                     buf.at[slot, k, pl.ds(dst0, yp), :], sems.at[slot])

    def gather_start(dests, slot):
        def issue(g, carry):
            for u in range(DMA_UNROLL // TOP_K):
                for k in range(TOP_K):
                    slab_copy(dests, g * (DMA_UNROLL // TOP_K) + u, k, slot).start(priority=k % 2)
            return carry
        lax.fori_loop(0, tm * TOP_K // DMA_UNROLL, issue, 0)

    def gather_wait(dests, slot):
        def drain(g, carry):
            for u in range(DMA_UNROLL // TOP_K):
                for k in range(TOP_K):
                    slab_copy(dests, g * (DMA_UNROLL // TOP_K) + u, k, slot).wait()
            return carry
        lax.fori_loop(0, tm * TOP_K // DMA_UNROLL, drain, 0)

    def compute(slot):
        w = wts_ref[...]
        wk = [jnp.broadcast_to(w[:, k:k + 1], (tm, LANES)) for k in range(TOP_K)]
        for c in range(yp):
            acc_lo = acc_hi = None
            for k in range(TOP_K):
                words = buf[slot, k, pl.ds(c, tm, stride=yp), :]
                lo = wk[k] * lax.bitcast_convert_type(lax.shift_left(words, jnp.uint32(16)), F32)
                hi = wk[k] * lax.bitcast_convert_type(words & jnp.uint32(0xFFFF0000), F32)
                acc_lo = lo if acc_lo is None else acc_lo + lo
                acc_hi = hi if acc_hi is None else acc_hi + hi
            for base, acc in ((0, acc_lo), (half, acc_hi)):
                sl = slice(base + c * LANES, base + (c + 1) * LANES)
                o_ref[:, sl] = x_ref[:, sl] + g_ref[0, :, sl] * acc

    @pl.when(i == 0)
    def _first():
        gather_start(dest_ref, 0)

    for slot in range(2):
        @pl.when((i % 2 == slot) & (i + 1 < n_steps))
        def _prefetch(slot=slot):
            gather_start(nxt_ref, 1 - slot)

        @pl.when(i % 2 == slot)
        def _active(slot=slot):
            gather_wait(dest_ref, slot)
            compute(slot)


def _combine(dest, ys, yp, wts, x2d, gate, *, seq, t):
    n_tok, d_model = x2d.shape
    tm = t["cmb_tm"]
    tpb = seq // tm
    n_steps = n_tok // tm
    return pl.pallas_call(
        functools.partial(_combine_kernel, tm=tm, yp=yp, half=d_model // 2, n_steps=n_steps),
        grid=(n_steps,),
        in_specs=[
            pl.BlockSpec((tm * TOP_K,), lambda i: (i,), memory_space=pltpu.SMEM),
            pl.BlockSpec((tm * TOP_K,), lambda i: (jnp.minimum(i + 1, n_steps - 1),),
                         memory_space=pltpu.SMEM),
            pl.BlockSpec(memory_space=pl.ANY),
            pl.BlockSpec((tm, LANES), lambda i: (i, 0)),
            pl.BlockSpec((tm, d_model), lambda i: (i, 0)),
            pl.BlockSpec((1, 1, d_model), lambda i: (i // tpb, 0, 0)),
        ],
        out_specs=pl.BlockSpec((tm, d_model), lambda i: (i, 0)),
        out_shape=jax.ShapeDtypeStruct((n_tok, d_model), F32),
        scratch_shapes=[pltpu.VMEM((2, TOP_K, tm * yp, LANES), ys.dtype), pltpu.SemaphoreType.DMA((2,))],
        compiler_params=_cparams("arbitrary"),
        name="moe_combine_residual",
    )(dest, dest, ys, wts, x2d, gate)


def _layer(layer, x2d, mod, norm1_g, w_in_f, w_in_all, forget_b, pool_w, pool_scale, q_norm_g, k_norm_g,
           w_out_all, norm2_g, router_w, router_b, w1_f32, b1, w2_f32, b2, *, seq, t):
    n_tok, d_model = x2d.shape
    batch = n_tok // seq
    n_groups, gd, _ = pool_w.shape
    pool_width = n_groups * gd
    att_width = d_model - pool_width
    n_heads = att_width // HEAD_DIM
    n_exp = router_w.shape[1]
    assert pool_width == att_width and n_groups == len(POOL_WINDOWS) and gd % HEAD_DIM == 0
    n_main = pool_width + 3 * att_width
    hp = -(-n_heads // BF16_ROWS) * BF16_ROWS

    shift1, scale1, gate1, shift2, scale2, gate2 = [
        m.reshape(batch, 1, d_model) for m in jnp.split(mod, N_MOD, axis=-1)]

    wf_src = lax.optimization_barrier(w_in_f[:, n_main:])
    wf_t = jnp.zeros((hp, d_model), BF16).at[:n_heads].set(wf_src.T.astype(BF16))
    fb = jnp.zeros((hp, 1), F32).at[:n_heads, 0].set(forget_b)
    reps = gd // HEAD_DIM
    h1 = _norm_modulate(x2d, norm1_g.reshape(1, d_model), scale1, shift1, seq=seq, t=t)
    main, cum = _in_projection(
        h1, w_in_all, layer, n_main, wf_t, fb,
        pool_w.astype(BF16), pool_scale.reshape(n_groups, 1, gd),
        jnp.tile(q_norm_g, reps).reshape(1, gd), jnp.tile(k_norm_g, reps).reshape(1, gd),
        seq=seq, t=t)
    att, w1, w2 = _attention(main, cum, w1_f32, w2_f32, layer, seq=seq, n_heads=n_heads,
                             q_off=pool_width, k_off=pool_width + att_width,
                             v_off=pool_width + 2 * att_width, t=t)
    x1 = _out_projection(main, att, w_out_all, layer, x2d, gate1, seq=seq,
                         pool_width=pool_width, t=t)

    rw = jnp.zeros((d_model, LANES), BF16).at[:, :n_exp].set(router_w.astype(BF16))
    rb = jnp.full((1, LANES), -jnp.inf, F32).at[0, :n_exp].set(router_b)
    packed, idx, wts, rank, counts = _router(
        x1, norm2_g.reshape(1, d_model), scale2, shift2, rw, rb, seq=seq, t=t)

    bm = t["moe_bm"]
    n_assign = n_tok * TOP_K
    n_blocks = -(-n_assign // bm) + n_exp
    counts = counts[0, :n_exp].astype(jnp.int32)
    padded = (counts + bm - 1) // bm * bm
    pad_end = jnp.cumsum(padded)
    pad_start = pad_end - padded
    dest = (pad_start[idx[:, :TOP_K]] + rank[:, :TOP_K]).reshape(-1)
    row_tok = jnp.zeros((n_blocks * bm,), jnp.int32).at[dest].set(
        jnp.arange(n_assign, dtype=jnp.int32) // TOP_K,
        unique_indices=True, mode="promise_in_bounds")
    block_start = jnp.arange(n_blocks, dtype=jnp.int32) * bm
    block_e = jnp.minimum(jnp.sum(block_start[:, None] >= pad_end[None, :], axis=1),
                          n_exp - 1).astype(jnp.int32)
    n_active = (pad_end[-1:] // bm).astype(jnp.int32)

    xp = d_model // 2 // LANES
    ys = _experts(block_e, n_active, row_tok, packed, w1, b1, w2, b2, bm=bm, xp=xp)
    return _combine(dest, ys, xp, wts, x1, gate2, seq=seq, t=t)


def kernel(x, c, ada_w, ada_b, norm1_g, w_in, forget_b, pool_w, pool_scale, q_norm_g, k_norm_g,
           w_out, norm2_g, router_w, router_b, expert_w1, expert_b1, expert_w2, expert_b2):
    batch, seq, d_model = x.shape
    depth = ada_w.shape[0]
    t = _tiles(batch * seq, seq, d_model)
    mod = _modulation(c, ada_w, ada_b, t["mod_tn"])
    x2d = x.reshape(batch * seq, d_model)
    w_in_all, w_out_all = w_in.astype(BF16), w_out.astype(BF16)
    for l in range(depth):
        x2d = _layer(l, x2d, mod[l], norm1_g[l], w_in[l], w_in_all, forget_b[l], pool_w[l], pool_scale[l],
                     q_norm_g[l], k_norm_g[l], w_out_all, norm2_g[l], router_w[l], router_b[l],
                     expert_w1, expert_b1[l], expert_w2, expert_b2[l], seq=seq, t=t)
    return x2d.reshape(batch, seq, d_model)
```

```python
import functools

import jax
import jax.numpy as jnp
from jax import lax
from jax.experimental import pallas as pl
from jax.experimental.pallas import tpu as pltpu

HEAD_DIM = 128
POOL_WINDOWS = (2, 4, 8, 16)
TOP_K = 4
N_MOD = 6
NORM_EPS = 1e-6
SWIGLU_ALPHA = 1.702
SWIGLU_LIMIT = 7.0
LOG2_E = 1.4426950408889634
ATT_LOGIT_SCALE = HEAD_DIM ** -0.5 * LOG2_E

LANES = 128
SUBLANES = 8
BF16_ROWS = 16
SCALAR_UNROLL = 8
DMA_UNROLL = 32
POOL_HALO = 16
VMEM_LIMIT_BYTES = 56 * 2**20

F32 = jnp.float32
BF16 = jnp.bfloat16


def _cparams(*sem):
    return pltpu.CompilerParams(dimension_semantics=sem, vmem_limit_bytes=VMEM_LIMIT_BYTES)


def _tiles(n_tok, seq, d_model):
    return dict(
        mod_tn=min(512, d_model),
        norm_tm=min(512, seq),
        in_tm=min(1024, seq),
        norm_rc=min(64, seq),
        att_tq=min(1024, seq),
        att_tk=512,
        att_heads=2,
        out_tm=min(1024, seq),
        out_tn=min(512, d_model),
        rt_tm=min(512, seq),
        moe_bm=min(256, n_tok),
        inv_blk=8192,
        cmb_tm=min(128, seq),
    )


def _mod_kernel(c_ref, w_ref, b_ref, o_ref):
    c = c_ref[...]
    ca = c / (1.0 + jnp.exp(-c))
    w = w_ref[0].astype(BF16)
    o_ref[0] = jnp.dot(ca.astype(BF16), w, preferred_element_type=F32) + b_ref[0]


def _modulation(c, ada_w, ada_b, tn):
    depth, d_model, n_out = ada_w.shape
    b = c.shape[0]
    c_pad = jnp.zeros((SUBLANES, d_model), F32).at[:b].set(c)
    out = pl.pallas_call(
        _mod_kernel,
        grid=(depth, n_out // tn),
        in_specs=[
            pl.BlockSpec((SUBLANES, d_model), lambda l, j: (0, 0)),
            pl.BlockSpec((1, d_model, tn), lambda l, j: (l, 0, j)),
            pl.BlockSpec((1, 1, tn), lambda l, j: (l, 0, j)),
        ],
        out_specs=pl.BlockSpec((1, SUBLANES, tn), lambda l, j: (l, 0, j)),
        out_shape=jax.ShapeDtypeStruct((depth, SUBLANES, n_out), F32),
        compiler_params=_cparams("arbitrary", "arbitrary"),
        name="adaln_mod",
    )(c_pad, ada_w, ada_b.reshape(depth, 1, n_out))
    return out[:, :b]


def _norm_kernel(x_ref, g_ref, sc_ref, sh_ref, h_ref, *, tm, rc):
    def body(c, carry):
        r0 = pl.multiple_of(c * rc, rc)
        x = x_ref[pl.ds(r0, rc), :]
        ms = jnp.mean(x * x, axis=-1, keepdims=True)
        y = x * lax.rsqrt(ms + NORM_EPS)
        h = (y * g_ref[...]) * (1.0 + sc_ref[0]) + sh_ref[0]
        h_ref[pl.ds(r0, rc), :] = h.astype(BF16)
        return carry
    lax.fori_loop(0, tm // rc, body, 0)


def _norm_modulate(x2d, gain, scale, shift, *, seq, t):
    n_tok, d_model = x2d.shape
    tm, rc = t["norm_tm"], t["norm_rc"]
    tpb = seq // tm
    return pl.pallas_call(
        functools.partial(_norm_kernel, tm=tm, rc=rc),
        grid=(n_tok // tm,),
        in_specs=[
            pl.BlockSpec((tm, d_model), lambda i: (i, 0)),
            pl.BlockSpec((1, d_model), lambda i: (0, 0)),
            pl.BlockSpec((1, 1, d_model), lambda i: (i // tpb, 0, 0)),
            pl.BlockSpec((1, 1, d_model), lambda i: (i // tpb, 0, 0)),
        ],
        out_specs=pl.BlockSpec((tm, d_model), lambda i: (i, 0)),
        out_shape=jax.ShapeDtypeStruct((n_tok, d_model), BF16),
        compiler_params=_cparams("arbitrary"),
        name="norm_modulate",
    )(x2d, gain, scale, shift)


def _in_kernel(h_ref, w_ref, wf_ref, fb_ref, pw_ref, ps_ref, qg_ref, kg_ref,
               main_ref, cum_ref, tail_sc, fc_sc, *, tm, tpb, gd):
    i = pl.program_id(0)
    j = pl.program_id(1)
    first = (i % tpb) == 0

    def proj():
        return jnp.dot(h_ref[...], w_ref[...], preferred_element_type=F32)

    for g, win in enumerate(POOL_WINDOWS):
        @pl.when(j == g)
        def _pool(g=g, win=win):
            u = proj()
            prev = jnp.where(first, 0.0, tail_sc[g])
            s = jnp.concatenate([prev, u], axis=0)
            shift = 1
            while shift < win:
                s = s + pltpu.roll(s, shift, 0)
                shift *= 2
            wsum = s[POOL_HALO:]
            pos = (i % tpb) * tm + lax.broadcasted_iota(jnp.int32, (tm, 1), 0)
            cnt = jnp.minimum(pos + 1, win).astype(F32)
            mixed = wsum / cnt - u
            po = jnp.dot(mixed.astype(BF16), pw_ref[0], preferred_element_type=F32) * ps_ref[0]
            main_ref[...] = po.astype(BF16)
            tail_sc[g] = u[tm - POOL_HALO:]

    def qk_norm(gain_ref, post_scale):
        r = proj()
        for c in range(gd // HEAD_DIM):
            sl = slice(c * HEAD_DIM, (c + 1) * HEAD_DIM)
            rc_ = r[:, sl]
            ms = jnp.mean(rc_ * rc_, axis=-1, keepdims=True)
            normed = (rc_ * lax.rsqrt(ms + NORM_EPS)) * gain_ref[:, sl]
            if post_scale is not None:
                normed = normed * post_scale
            main_ref[:, sl] = normed.astype(BF16)

    @pl.when((j >= 4) & (j < 8))
    def _q():
        qk_norm(qg_ref, ATT_LOGIT_SCALE)

    @pl.when((j >= 8) & (j < 12))
    def _k():
        qk_norm(kg_ref, None)

    @pl.when((j >= 12) & (j < 16))
    def _v():
        main_ref[...] = proj().astype(BF16)

    @pl.when(j == 16)
    def _forget():
        z = lax.dot_general(wf_ref[...], h_ref[...], (((1,), (1,)), ((), ())),
                            preferred_element_type=F32) + fb_ref[...]
        ls = jnp.minimum(z, 0.0) - jnp.log1p(jnp.exp(-jnp.abs(z)))
        p0 = ls.astype(BF16)
        r1 = ls - p0.astype(F32)
        p1 = r1.astype(BF16)
        p2 = (r1 - p1.astype(F32)).astype(BF16)
        rr = lax.broadcasted_iota(jnp.int32, (tm, tm), 0)
        cc = lax.broadcasted_iota(jnp.int32, (tm, tm), 1)
        tri = jnp.where(rr <= cc, 1.0, 0.0).astype(BF16)
        cum = (jnp.dot(p0, tri, preferred_element_type=F32)
               + jnp.dot(p1, tri, preferred_element_type=F32)
               + jnp.dot(p2, tri, preferred_element_type=F32))
        cum = cum + jnp.where(first, 0.0, fc_sc[:, 0:1])
        cum_ref[0] = cum * LOG2_E
        fc_sc[...] = jnp.broadcast_to(cum[:, tm - 1:tm], fc_sc.shape)


def _in_projection(h2d, w_in_all, layer, n_main, wf_t, fb, pool_w, pool_scale, qg, kg, *, seq, t):
    n_tok, d_model = h2d.shape
    batch = n_tok // seq
    tm = t["in_tm"]
    gd = pool_w.shape[-1]
    hp = wf_t.shape[0]
    tpb = seq // tm
    nj = n_main // gd + 1
    last = n_main // gd - 1
    kern = functools.partial(_in_kernel, tm=tm, tpb=tpb, gd=gd)
    return pl.pallas_call(
        kern,
        grid=(n_tok // tm, nj),
        in_specs=[
            pl.BlockSpec((tm, d_model), lambda i, j: (i, 0)),
            pl.BlockSpec((None, d_model, gd), lambda i, j: (layer, 0, jnp.minimum(j, last))),
            pl.BlockSpec((hp, d_model), lambda i, j: (0, 0)),
            pl.BlockSpec((hp, 1), lambda i, j: (0, 0)),
            pl.BlockSpec((1, gd, gd), lambda i, j: (jnp.minimum(j, 3), 0, 0)),
            pl.BlockSpec((1, 1, gd), lambda i, j: (jnp.minimum(j, 3), 0, 0)),
            pl.BlockSpec((1, gd), lambda i, j: (0, 0)),
            pl.BlockSpec((1, gd), lambda i, j: (0, 0)),
        ],
        out_specs=[
            pl.BlockSpec((tm, gd), lambda i, j: (i, jnp.minimum(j, last))),
            pl.BlockSpec((1, hp, tm), lambda i, j: (i // tpb, 0, i % tpb)),
        ],
        out_shape=[
            jax.ShapeDtypeStruct((n_tok, n_main), BF16),
            jax.ShapeDtypeStruct((batch, hp, seq), F32),
        ],
        scratch_shapes=[
            pltpu.VMEM((len(POOL_WINDOWS), POOL_HALO, gd), F32),
            pltpu.VMEM((hp, LANES), F32),
        ],
        compiler_params=_cparams("arbitrary", "arbitrary"),
        name="in_proj",
    )(h2d, w_in_all, wf_t, fb, pool_w, pool_scale, qg, kg)


def _attn_kernel(q_ref, k_ref, v_ref, c_ref, w1_ref, w2_ref, o_ref, w1o_ref, w2o_ref, *, tq, tk, hps):
    w1o_ref[...] = w1_ref[...].astype(BF16)
    w2o_ref[...] = w2_ref[...].astype(BF16)
    qi = pl.program_id(2)
    heads = [slice(h * HEAD_DIM, (h + 1) * HEAD_DIM) for h in range(hps)]
    kpq = tq // tk

    def block(h, kj, carry, diag_index):
        m_prev, l_prev, acc_prev = carry
        start = pl.multiple_of(kj * tk, tk)
        k = k_ref[pl.ds(start, tk), heads[h]]
        v = v_ref[pl.ds(start, tk), heads[h]]
        s = lax.dot_general(q_ref[:, heads[h]], k, (((1,), (1,)), ((), ())),
                            preferred_element_type=F32)
        s = s - c_ref[h, :, pl.ds(start, tk)]
        if diag_index is not None:
            rr = lax.broadcasted_iota(jnp.int32, (tq, tk), 0)
            cc = lax.broadcasted_iota(jnp.int32, (tq, tk), 1)
            s = jnp.where(cc + diag_index * tk <= rr, s, -jnp.inf)
        m_new = jnp.maximum(m_prev, jnp.max(s, axis=-1, keepdims=True))
        p = jnp.exp2(s - m_new)
        alpha = jnp.exp2(m_prev - m_new)
        l_new = alpha * l_prev + jnp.sum(p, axis=-1, keepdims=True)
        acc_new = alpha * acc_prev + jnp.dot(p.astype(BF16), v, preferred_element_type=F32)
        return m_new, l_new, acc_new

    def full_blocks(g, carry):
        for d in range(kpq):
            carry = tuple(block(h, g * kpq + d, carry[h], None) for h in range(hps))
        return carry

    def diagonal_blocks(_, carry):
        for d in range(kpq):
            carry = tuple(block(h, qi * kpq + d, carry[h], d) for h in range(hps))
        return carry

    init = tuple((jnp.full((tq, 1), -jnp.inf, F32), jnp.zeros((tq, 1), F32),
                  jnp.zeros((tq, HEAD_DIM), F32)) for _ in range(hps))
    carry = lax.fori_loop(0, qi, full_blocks, init)
    carry = lax.fori_loop(0, jnp.minimum(qi + 1, 1), diagonal_blocks, carry)
    for h in range(hps):
        _, l_f, acc_f = carry[h]
        o_ref[:, heads[h]] = (acc_f / l_f).astype(o_ref.dtype)


def _attention(main, cum, w1_f32, w2_f32, layer, *, seq, n_heads, q_off, k_off, v_off, t):
    n_tok = main.shape[0]
    batch = n_tok // seq
    tq, hps = t["att_tq"], min(t["att_heads"], n_heads)
    nq = seq // tq
    hp = cum.shape[1]
    width = hps * HEAD_DIM
    assert n_heads % hps == 0 and hp % hps == 0
    cum3 = cum.reshape(batch * hp, 1, seq)
    kern = functools.partial(_attn_kernel, tq=tq, tk=min(t["att_tk"], tq), hps=hps)
    qb, kb, vb, cb = q_off // width, k_off // width, v_off // width, hp // hps
    n_hg = n_heads // hps
    n_steps = batch * n_hg * nq
    depth, n_exp, d_model, two_f = w1_f32.shape
    fdim = w2_f32.shape[2]
    rows1, rows2 = n_exp * d_model, n_exp * fdim
    assert rows1 % (n_steps * BF16_ROWS) == 0 and rows2 % (n_steps * BF16_ROWS) == 0
    r1, r2 = rows1 // n_steps, rows2 // n_steps

    def step(b, h, i):
        return (b * n_hg + h) * nq + i

    att, w1b, w2b = pl.pallas_call(
        kern,
        grid=(batch, n_hg, nq),
        in_specs=[
            pl.BlockSpec((tq, width), lambda b, h, i: (b * nq + i, qb + h)),
            pl.BlockSpec((seq, width), lambda b, h, i: (b, kb + h)),
            pl.BlockSpec((seq, width), lambda b, h, i: (b, vb + h)),
            pl.BlockSpec((hps, 1, seq), lambda b, h, i: (b * cb + h, 0, 0)),
            pl.BlockSpec((None, r1, two_f), lambda b, h, i: (layer, step(b, h, i), 0)),
            pl.BlockSpec((None, r2, d_model), lambda b, h, i: (layer, step(b, h, i), 0)),
        ],
        out_specs=[
            pl.BlockSpec((tq, width), lambda b, h, i: (b * nq + i, h)),
            pl.BlockSpec((r1, two_f), lambda b, h, i: (step(b, h, i), 0)),
            pl.BlockSpec((r2, d_model), lambda b, h, i: (step(b, h, i), 0)),
        ],
        out_shape=[
            jax.ShapeDtypeStruct((n_tok, n_heads * HEAD_DIM), BF16),
            jax.ShapeDtypeStruct((rows1, two_f), BF16),
            jax.ShapeDtypeStruct((rows2, d_model), BF16),
        ],
        compiler_params=_cparams("arbitrary", "arbitrary", "arbitrary"),
        name="forget_attention",
    )(main, main, main, cum3, w1_f32.reshape(depth, rows1, two_f), w2_f32.reshape(depth, rows2, d_model))
    return att, w1b.reshape(n_exp, d_model, two_f), w2b.reshape(n_exp, fdim, d_model)


def _out_kernel(po_ref, at_ref, wp_ref, wa_ref, x_ref, g_ref, o_ref):
    mix = (jnp.dot(po_ref[...], wp_ref[...], preferred_element_type=F32)
           + jnp.dot(at_ref[...], wa_ref[...], preferred_element_type=F32))
    o_ref[...] = x_ref[...] + g_ref[0] * mix


def _out_projection(main, att, w_out_all, layer, x2d, gate, *, seq, pool_width, t):
    n_tok, d_model = x2d.shape
    tm, tn = t["out_tm"], t["out_tn"]
    att_width = att.shape[1]
    tpb = seq // tm
    return pl.pallas_call(
        _out_kernel,
        grid=(n_tok // tm, d_model // tn),
        in_specs=[
            pl.BlockSpec((tm, pool_width), lambda i, j: (i, 0)),
            pl.BlockSpec((tm, att_width), lambda i, j: (i, 0)),
            pl.BlockSpec((None, pool_width, tn), lambda i, j: (layer, 0, j)),
            pl.BlockSpec((None, att_width, tn), lambda i, j: (layer, pool_width // att_width, j)),
            pl.BlockSpec((tm, tn), lambda i, j: (i, j)),
            pl.BlockSpec((1, 1, tn), lambda i, j: (i // tpb, 0, j)),
        ],
        out_specs=pl.BlockSpec((tm, tn), lambda i, j: (i, j)),
        out_shape=jax.ShapeDtypeStruct((n_tok, d_model), F32),
        compiler_params=_cparams("arbitrary", "arbitrary"),
        name="out_proj_residual",
    )(main, att, w_out_all, w_out_all, x2d, gate)


def _router_kernel(x_ref, g_ref, sc_ref, sh_ref, rw_ref, rb_ref,
                   hp_ref, idx_ref, wts_ref, rank_ref, cnt_ref, h_sc, cnt_sc, *, tm, rc, half):
    i = pl.program_id(0)

    @pl.when(i == 0)
    def _init():
        cnt_sc[...] = jnp.zeros(cnt_sc.shape, F32)

    def body(c, carry):
        r0 = pl.multiple_of(c * rc, rc)
        x = x_ref[pl.ds(r0, rc), :]
        ms = jnp.mean(x * x, axis=-1, keepdims=True)
        y = x * lax.rsqrt(ms + NORM_EPS)
        h = ((y * g_ref[...]) * (1.0 + sc_ref[0]) + sh_ref[0]).astype(BF16)
        h_sc[pl.ds(r0, rc), :] = h
        bits = lax.bitcast_convert_type(h.astype(F32), jnp.uint32)
        lo = lax.shift_right_logical(bits[:, :half], jnp.uint32(16))
        hi = bits[:, half:] & jnp.uint32(0xFFFF0000)
        words = lo | hi
        pitch = half // LANES
        for s in range(pitch):
            hp_ref[pl.ds(r0 * pitch + s, rc, stride=pitch), :] = words[:, s * LANES:(s + 1) * LANES]
        return carry
    lax.fori_loop(0, tm // rc, body, 0)

    logits = jnp.dot(h_sc[...], rw_ref[...], preferred_element_type=F32) + rb_ref[...]
    lane = lax.broadcasted_iota(jnp.int32, logits.shape, 1).astype(F32)
    vals, sels, hots = [], [], []
    cur = logits
    for _ in range(TOP_K):
        mx = jnp.max(cur, axis=-1, keepdims=True)
        sel = jnp.min(jnp.where(cur == mx, lane, float(LANES)), axis=-1, keepdims=True)
        hot = lane == sel
        vals.append(mx)
        sels.append(sel)
        hots.append(hot)
        cur = jnp.where(hot, -jnp.inf, cur)
    exps = [jnp.exp(v - vals[0]) for v in vals]
    denom = exps[0] + exps[1] + exps[2] + exps[3]

    hot_sum = jnp.zeros(logits.shape, F32)
    for hot in hots:
        hot_sum = hot_sum + jnp.where(hot, 1.0, 0.0)
    rr = lax.broadcasted_iota(jnp.int32, (tm, tm), 0)
    cc = lax.broadcasted_iota(jnp.int32, (tm, tm), 1)
    below = jnp.where(cc < rr, 1.0, 0.0).astype(BF16)
    base = jnp.dot(below, hot_sum.astype(BF16), preferred_element_type=F32) + cnt_sc[...]

    idx_out = jnp.zeros(logits.shape, F32)
    wts_out = jnp.zeros(logits.shape, F32)
    rank_out = jnp.zeros(logits.shape, F32)
    for k in range(TOP_K):
        col = lane == float(k)
        rank_k = jnp.sum(jnp.where(hots[k], base, 0.0), axis=-1, keepdims=True)
        idx_out = jnp.where(col, sels[k], idx_out)
        wts_out = jnp.where(col, exps[k] / denom, wts_out)
        rank_out = jnp.where(col, rank_k, rank_out)
    idx_ref[...] = idx_out.astype(jnp.int32)
    wts_ref[...] = wts_out
    rank_ref[...] = rank_out.astype(jnp.int32)
    total = cnt_sc[...] + jnp.sum(hot_sum, axis=0, keepdims=True)
    cnt_sc[...] = total
    cnt_ref[...] = total


def _router(x2d, gain, scale, shift, rw, rb, *, seq, t):
    n_tok, d_model = x2d.shape
    tm, rc = t["rt_tm"], t["norm_rc"]
    tpb = seq // tm
    half = d_model // 2
    kern = functools.partial(_router_kernel, tm=tm, rc=rc, half=half)
    return pl.pallas_call(
        kern,
        grid=(n_tok // tm,),
        in_specs=[
            pl.BlockSpec((tm, d_model), lambda i: (i, 0)),
            pl.BlockSpec((1, d_model), lambda i: (0, 0)),
            pl.BlockSpec((1, 1, d_model), lambda i: (i // tpb, 0, 0)),
            pl.BlockSpec((1, 1, d_model), lambda i: (i // tpb, 0, 0)),
            pl.BlockSpec((d_model, LANES), lambda i: (0, 0)),
            pl.BlockSpec((1, LANES), lambda i: (0, 0)),
        ],
        out_specs=[
            pl.BlockSpec((tm * (half // LANES), LANES), lambda i: (i, 0)),
            pl.BlockSpec((tm, LANES), lambda i: (i, 0)),
            pl.BlockSpec((tm, LANES), lambda i: (i, 0)),
            pl.BlockSpec((tm, LANES), lambda i: (i, 0)),
            pl.BlockSpec((1, LANES), lambda i: (0, 0)),
        ],
        out_shape=[
            jax.ShapeDtypeStruct((n_tok * (half // LANES), LANES), jnp.uint32),
            jax.ShapeDtypeStruct((n_tok, LANES), jnp.int32),
            jax.ShapeDtypeStruct((n_tok, LANES), F32),
            jax.ShapeDtypeStruct((n_tok, LANES), jnp.int32),
            jax.ShapeDtypeStruct((1, LANES), F32),
        ],
        scratch_shapes=[pltpu.VMEM((tm, d_model), BF16), pltpu.VMEM((1, LANES), F32)],
        compiler_params=_cparams("arbitrary"),
        name="norm_router_topk",
    )(x2d, gain, scale, shift, rw, rb)


def _row_token_kernel(dest_ref, zeros_ref, o_ref, sem, *, blk):
    i = pl.program_id(0)

    @pl.when(i == 0)
    def _init():
        fill = pltpu.make_async_copy(zeros_ref, o_ref, sem)
        fill.start()
        fill.wait()

    base = i * blk

    def place(g, carry):
        for u in range(SCALAR_UNROLL):
            a = g * SCALAR_UNROLL + u
            o_ref[dest_ref[a]] = lax.shift_right_logical(base + a, TOP_K.bit_length() - 1)
        return carry
    lax.fori_loop(0, blk // SCALAR_UNROLL, place, 0)


def _row_tokens(dest, n_rows, t):
    n_assign = dest.shape[0]
    blk = min(t["inv_blk"], n_assign)
    assert TOP_K & (TOP_K - 1) == 0 and n_assign % blk == 0 and blk % SCALAR_UNROLL == 0
    return pl.pallas_call(
        functools.partial(_row_token_kernel, blk=blk),
        grid=(n_assign // blk,),
        in_specs=[pl.BlockSpec((blk,), lambda i: (i,), memory_space=pltpu.SMEM),
                  pl.BlockSpec(memory_space=pl.ANY)],
        out_specs=pl.BlockSpec(memory_space=pltpu.SMEM),
        out_shape=jax.ShapeDtypeStruct((n_rows,), jnp.int32),
        scratch_shapes=[pltpu.SemaphoreType.DMA],
        compiler_params=_cparams("arbitrary"),
        name="moe_row_tokens",
    )(dest, jnp.zeros((n_rows,), jnp.int32))


def _expert_kernel(be_ref, na_ref, tok_ref, nxt_ref, src_ref, w1_ref, b1_ref, w2_ref, b2_ref,
                   o_ref, xbuf, sems, *, bm, half, fdim, xp):
    del be_ref
    i = pl.program_id(0)
    n_active = na_ref[0]

    def slab_copy(toks, r, slot):
        src0 = pl.multiple_of(toks[r] * xp, xp)
        dst0 = pl.multiple_of(r * xp, xp)
        return pltpu.make_async_copy(src_ref.at[pl.ds(src0, xp), :],
                                     xbuf.at[slot, pl.ds(dst0, xp), :], sems.at[slot])

    def gather_start(toks, slot):
        def issue(g, carry):
            for u in range(DMA_UNROLL):
                slab_copy(toks, g * DMA_UNROLL + u, slot).start(priority=u % 2)
            return carry
        lax.fori_loop(0, bm // DMA_UNROLL, issue, 0)

    def gather_wait(toks, slot):
        def drain(g, carry):
            for u in range(DMA_UNROLL):
                slab_copy(toks, g * DMA_UNROLL + u, slot).wait()
            return carry
        lax.fori_loop(0, bm // DMA_UNROLL, drain, 0)

    def compute(slot):
        words = jnp.concatenate([xbuf[slot, pl.ds(s, bm, stride=xp), :] for s in range(xp)], axis=1)
        lo = lax.bitcast_convert_type(lax.shift_left(words, jnp.uint32(16)), F32).astype(BF16)
        hi = lax.bitcast_convert_type(words & jnp.uint32(0xFFFF0000), F32).astype(BF16)
        hid = (jnp.dot(lo, w1_ref[0, :half, :], preferred_element_type=F32)
               + jnp.dot(hi, w1_ref[0, half:, :], preferred_element_type=F32)) + b1_ref[0]
        glu = jnp.minimum(hid[:, :fdim], SWIGLU_LIMIT)
        lin = jnp.clip(hid[:, fdim:], -SWIGLU_LIMIT, SWIGLU_LIMIT)
        act = (glu / (1.0 + jnp.exp(-SWIGLU_ALPHA * glu))) * (lin + 1.0)
        y = jnp.dot(act.astype(BF16), w2_ref[0], preferred_element_type=F32) + b2_ref[0]
        bits = lax.bitcast_convert_type(y.astype(BF16).astype(F32), jnp.uint32)
        out_words = lax.shift_right_logical(bits[:, :half], jnp.uint32(16)) | (
            bits[:, half:] & jnp.uint32(0xFFFF0000))
        for c in range(xp):
            o_ref[pl.ds(c, bm, stride=xp), :] = out_words[:, c * LANES:(c + 1) * LANES]

    @pl.when(i == 0)
    def _first():
        gather_start(tok_ref, 0)

    for slot in range(2):
        @pl.when((i % 2 == slot) & (i + 1 < n_active))
        def _prefetch(slot=slot):
            gather_start(nxt_ref, 1 - slot)

        @pl.when((i % 2 == slot) & (i < n_active))
        def _active(slot=slot):
            gather_wait(tok_ref, slot)
            compute(slot)

    @pl.when(i >= n_active)
    def _unused():
        o_ref[...] = jnp.zeros(o_ref.shape, o_ref.dtype)


def _experts(block_e, n_active, row_tok, packed, w1, b1, w2, b2, *, bm, xp):
    n_blocks = row_tok.shape[0] // bm
    n_exp, d_model, two_f = w1.shape
    fdim, half = two_f // 2, d_model // 2
    grid_spec = pltpu.PrefetchScalarGridSpec(
        num_scalar_prefetch=2,
        grid=(n_blocks,),
        in_specs=[
            pl.BlockSpec((bm,), lambda i, be, na: (i,), memory_space=pltpu.SMEM),
            pl.BlockSpec((bm,), lambda i, be, na: (jnp.minimum(i + 1, n_blocks - 1),),
                         memory_space=pltpu.SMEM),
            pl.BlockSpec(memory_space=pl.ANY),
            pl.BlockSpec((1, d_model, two_f), lambda i, be, na: (be[i], 0, 0)),
            pl.BlockSpec((1, 1, two_f), lambda i, be, na: (be[i], 0, 0)),
            pl.BlockSpec((1, fdim, d_model), lambda i, be, na: (be[i], 0, 0)),
            pl.BlockSpec((1, 1, d_model), lambda i, be, na: (be[i], 0, 0)),
        ],
        out_specs=pl.BlockSpec((bm * xp, LANES), lambda i, be, na: (i, 0)),
        scratch_shapes=[pltpu.VMEM((2, bm * xp, LANES), packed.dtype),
                        pltpu.SemaphoreType.DMA((2,))],
    )
    return pl.pallas_call(
        functools.partial(_expert_kernel, bm=bm, half=half, fdim=fdim, xp=xp),
        grid_spec=grid_spec,
        out_shape=jax.ShapeDtypeStruct((n_blocks * bm * xp, LANES), packed.dtype),
        compiler_params=_cparams("arbitrary"),
        name="moe_experts",
    )(block_e, n_active, row_tok, row_tok, packed, w1, b1.reshape(n_exp, 1, two_f),
      w2, b2.reshape(n_exp, 1, d_model))


def _combine_kernel(dest_ref, nxt_ref, y_ref, wts_ref, x_ref, g_ref, *rest, tm, yp, half, n_steps,
                    next_norm):
    if next_norm:
        ng_ref, nsc_ref, nsh_ref, o_ref, hn_ref, buf, sems = rest
    else:
        o_ref, buf, sems = rest
    i = pl.program_id(0)

    def slab_copy(dests, r, k, slot):
        src0 = pl.multiple_of(dests[r * TOP_K + k] * yp, yp)
        dst0 = pl.multiple_of(r * yp, yp)
        return pltpu.make_async_copy(y_ref.at[pl.ds(src0, yp), :],
                                     buf.at[slot, k, pl.ds(dst0, yp), :], sems.at[slot])

    def gather_start(dests, slot):
        def issue(g, carry):
            for u in range(DMA_UNROLL // TOP_K):
                for k in range(TOP_K):
                    slab_copy(dests, g * (DMA_UNROLL // TOP_K) + u, k, slot).start(priority=k % 2)
            return carry
        lax.fori_loop(0, tm * TOP_K // DMA_UNROLL, issue, 0)

    def gather_wait(dests, slot):
        def drain(g, carry):
            for u in range(DMA_UNROLL // TOP_K):
                for k in range(TOP_K):
                    slab_copy(dests, g * (DMA_UNROLL // TOP_K) + u, k, slot).wait()
            return carry
        lax.fori_loop(0, tm * TOP_K // DMA_UNROLL, drain, 0)

    def compute(slot):
        w = wts_ref[...]
        wk = [jnp.broadcast_to(w[:, k:k + 1], (tm, LANES)) for k in range(TOP_K)]
        for c in range(yp):
            acc_lo = acc_hi = None
            for k in range(TOP_K):
                words = buf[slot, k, pl.ds(c, tm, stride=yp), :]
                lo = wk[k] * lax.bitcast_convert_type(lax.shift_left(words, jnp.uint32(16)), F32)
                hi = wk[k] * lax.bitcast_convert_type(words & jnp.uint32(0xFFFF0000), F32)
                acc_lo = lo if acc_lo is None else acc_lo + lo
                acc_hi = hi if acc_hi is None else acc_hi + hi
            for base, acc in ((0, acc_lo), (half, acc_hi)):
                sl = slice(base + c * LANES, base + (c + 1) * LANES)
                o_ref[:, sl] = x_ref[:, sl] + g_ref[0, :, sl] * acc
        if next_norm:
            x_new = o_ref[...]
            ms = jnp.mean(x_new * x_new, axis=-1, keepdims=True)
            y = x_new * lax.rsqrt(ms + NORM_EPS)
            hn_ref[...] = ((y * ng_ref[...]) * (1.0 + nsc_ref[0]) + nsh_ref[0]).astype(BF16)

    @pl.when(i == 0)
    def _first():
        gather_start(dest_ref, 0)

    for slot in range(2):
        @pl.when((i % 2 == slot) & (i + 1 < n_steps))
        def _prefetch(slot=slot):
            gather_start(nxt_ref, 1 - slot)

        @pl.when(i % 2 == slot)
        def _active(slot=slot):
            gather_wait(dest_ref, slot)
            compute(slot)


def _combine(dest, ys, yp, wts, x2d, gate, next_norm, *, seq, t):
    n_tok, d_model = x2d.shape
    tm = t["cmb_tm"]
    tpb = seq // tm
    n_steps = n_tok // tm
    row_spec = pl.BlockSpec((tm, d_model), lambda i: (i, 0))
    mod_spec = pl.BlockSpec((1, 1, d_model), lambda i: (i // tpb, 0, 0))
    in_specs = [
        pl.BlockSpec((tm * TOP_K,), lambda i: (i,), memory_space=pltpu.SMEM),
        pl.BlockSpec((tm * TOP_K,), lambda i: (jnp.minimum(i + 1, n_steps - 1),),
                     memory_space=pltpu.SMEM),
        pl.BlockSpec(memory_space=pl.ANY),
        pl.BlockSpec((tm, LANES), lambda i: (i, 0)),
        row_spec,
        mod_spec,
    ]
    out_specs, out_shape = [row_spec], [jax.ShapeDtypeStruct((n_tok, d_model), F32)]
    args = [dest, dest, ys, wts, x2d, gate]
    if next_norm is not None:
        in_specs += [pl.BlockSpec((1, d_model), lambda i: (0, 0)), mod_spec, mod_spec]
        out_specs.append(row_spec)
        out_shape.append(jax.ShapeDtypeStruct((n_tok, d_model), BF16))
        args += list(next_norm)
    outs = pl.pallas_call(
        functools.partial(_combine_kernel, tm=tm, yp=yp, half=d_model // 2, n_steps=n_steps,
                          next_norm=next_norm is not None),
        grid=(n_steps,),
        in_specs=in_specs,
        out_specs=out_specs,
        out_shape=out_shape,
        scratch_shapes=[pltpu.VMEM((2, TOP_K, tm * yp, LANES), ys.dtype), pltpu.SemaphoreType.DMA((2,))],
        compiler_params=_cparams("arbitrary"),
        name="moe_combine_residual",
    )(*args)
    return (outs[0], outs[1]) if next_norm is not None else (outs[0], None)


def _layer(layer, x2d, h1, mod, next_norm, w_in_f, w_in_all, forget_b, pool_w, pool_scale, q_norm_g,
           k_norm_g, w_out_all, norm2_g, router_w, router_b, w1_f32, b1, w2_f32, b2, *, seq, t):
    n_tok, d_model = x2d.shape
    batch = n_tok // seq
    n_groups, gd, _ = pool_w.shape
    pool_width = n_groups * gd
    att_width = d_model - pool_width
    n_heads = att_width // HEAD_DIM
    n_exp = router_w.shape[1]
    assert pool_width == att_width and n_groups == len(POOL_WINDOWS) and gd % HEAD_DIM == 0
    n_main = pool_width + 3 * att_width
    hp = -(-n_heads // BF16_ROWS) * BF16_ROWS

    shift1, scale1, gate1, shift2, scale2, gate2 = [
        m.reshape(batch, 1, d_model) for m in jnp.split(mod, N_MOD, axis=-1)]

    wf_src = lax.optimization_barrier(w_in_f[:, n_main:])
    wf_t = jnp.zeros((hp, d_model), BF16).at[:n_heads].set(wf_src.T.astype(BF16))
    fb = jnp.zeros((hp, 1), F32).at[:n_heads, 0].set(forget_b)
    reps = gd // HEAD_DIM
    main, cum = _in_projection(
        h1, w_in_all, layer, n_main, wf_t, fb,
        pool_w.astype(BF16), pool_scale.reshape(n_groups, 1, gd),
        jnp.tile(q_norm_g, reps).reshape(1, gd), jnp.tile(k_norm_g, reps).reshape(1, gd),
        seq=seq, t=t)
    att, w1, w2 = _attention(main, cum, w1_f32, w2_f32, layer, seq=seq, n_heads=n_heads,
                             q_off=pool_width, k_off=pool_width + att_width,
                             v_off=pool_width + 2 * att_width, t=t)
    x1 = _out_projection(main, att, w_out_all, layer, x2d, gate1, seq=seq,
                         pool_width=pool_width, t=t)

    rw = jnp.zeros((d_model, LANES), BF16).at[:, :n_exp].set(router_w.astype(BF16))
    rb = jnp.full((1, LANES), -jnp.inf, F32).at[0, :n_exp].set(router_b)
    packed, idx, wts, rank, counts = _router(
        x1, norm2_g.reshape(1, d_model), scale2, shift2, rw, rb, seq=seq, t=t)

    bm = t["moe_bm"]
    n_assign = n_tok * TOP_K
    n_blocks = -(-n_assign // bm) + n_exp
    counts = counts[0, :n_exp].astype(jnp.int32)
    padded = (counts + bm - 1) // bm * bm
    pad_end = jnp.cumsum(padded)
    pad_start = pad_end - padded
    dest = (pad_start[idx[:, :TOP_K]] + rank[:, :TOP_K]).reshape(-1)
    row_tok = _row_tokens(dest, n_blocks * bm, t)
    block_start = jnp.arange(n_blocks, dtype=jnp.int32) * bm
    block_e = jnp.minimum(jnp.sum(block_start[:, None] >= pad_end[None, :], axis=1),
                          n_exp - 1).astype(jnp.int32)
    n_active = (pad_end[-1:] // bm).astype(jnp.int32)

    xp = d_model // 2 // LANES
    ys = _experts(block_e, n_active, row_tok, packed, w1, b1, w2, b2, bm=bm, xp=xp)
    return _combine(dest, ys, xp, wts, x1, gate2, next_norm, seq=seq, t=t)


def kernel(x, c, ada_w, ada_b, norm1_g, w_in, forget_b, pool_w, pool_scale, q_norm_g, k_norm_g,
           w_out, norm2_g, router_w, router_b, expert_w1, expert_b1, expert_w2, expert_b2):
    batch, seq, d_model = x.shape
    depth = ada_w.shape[0]
    t = _tiles(batch * seq, seq, d_model)
    mod = _modulation(c, ada_w, ada_b, t["mod_tn"])
    x2d = x.reshape(batch * seq, d_model)
    w_in_all, w_out_all = w_in.astype(BF16), w_out.astype(BF16)

    def first_norm(l):
        shift1, scale1 = jnp.split(mod[l], N_MOD, axis=-1)[:2]
        return (norm1_g[l].reshape(1, d_model), scale1.reshape(batch, 1, d_model),
                shift1.reshape(batch, 1, d_model))

    h1 = _norm_modulate(x2d, *first_norm(0), seq=seq, t=t)
    for l in range(depth):
        next_norm = first_norm(l + 1) if l + 1 < depth else None
        x2d, h1 = _layer(l, x2d, h1, mod[l], next_norm, w_in[l], w_in_all, forget_b[l], pool_w[l],
                         pool_scale[l], q_norm_g[l], k_norm_g[l], w_out_all, norm2_g[l], router_w[l],
                         router_b[l], expert_w1, expert_b1[l], expert_w2, expert_b2[l], seq=seq, t=t)
    return x2d.reshape(batch, seq, d_model)
```

```python
import functools

import jax
import jax.numpy as jnp
from jax import lax
from jax.experimental import pallas as pl
from jax.experimental.pallas import tpu as pltpu

HEAD_DIM = 128
POOL_WINDOWS = (2, 4, 8, 16)
TOP_K = 4
N_MOD = 6
NORM_EPS = 1e-6
SWIGLU_ALPHA = 1.702
SWIGLU_LIMIT = 7.0
LOG2_E = 1.4426950408889634
ATT_LOGIT_SCALE = HEAD_DIM ** -0.5 * LOG2_E

LANES = 128
SUBLANES = 8
BF16_ROWS = 16
SCALAR_UNROLL = 8
DMA_UNROLL = 32
POOL_HALO = 16
VMEM_LIMIT_BYTES = 56 * 2**20

F32 = jnp.float32
BF16 = jnp.bfloat16


def _cparams(*sem):
    return pltpu.CompilerParams(dimension_semantics=sem, vmem_limit_bytes=VMEM_LIMIT_BYTES)


def _tiles(n_tok, seq, d_model):
    return dict(
        mod_tn=min(512, d_model),
        cast_rows=min(256, d_model),
        norm_tm=min(512, seq),
        in_tm=min(1024, seq),
        norm_rc=min(64, seq),
        att_tq=min(1024, seq),
        att_tk=512,
        att_heads=2,
        out_tm=min(1024, seq),
        out_tn=min(512, d_model),
        rt_tm=min(512, seq),
        moe_bm=min(256, n_tok),
        inv_blk=8192,
        cmb_tm=min(128, seq),
    )


def _cast_kernel(x_ref, o_ref):
    o_ref[...] = x_ref[...].astype(BF16)


def _cast_bf16(w, rows):
    depth, n_rows, n_cols = w.shape
    spec = pl.BlockSpec((1, rows, n_cols), lambda l, i: (l, i, 0))
    return pl.pallas_call(
        _cast_kernel,
        grid=(depth, n_rows // rows),
        in_specs=[spec],
        out_specs=spec,
        out_shape=jax.ShapeDtypeStruct(w.shape, BF16),
        compiler_params=_cparams("arbitrary", "arbitrary"),
        name="weight_cast",
    )(w)


def _mod_kernel(c_ref, w_ref, b_ref, o_ref):
    c = c_ref[...]
    ca = c / (1.0 + jnp.exp(-c))
    w = w_ref[0].astype(BF16)
    o_ref[0] = jnp.dot(ca.astype(BF16), w, preferred_element_type=F32) + b_ref[0]


def _modulation(c, ada_w, ada_b, tn):
    depth, d_model, n_out = ada_w.shape
    b = c.shape[0]
    c_pad = jnp.zeros((SUBLANES, d_model), F32).at[:b].set(c)
    out = pl.pallas_call(
        _mod_kernel,
        grid=(depth, n_out // tn),
        in_specs=[
            pl.BlockSpec((SUBLANES, d_model), lambda l, j: (0, 0)),
            pl.BlockSpec((1, d_model, tn), lambda l, j: (l, 0, j)),
            pl.BlockSpec((1, 1, tn), lambda l, j: (l, 0, j)),
        ],
        out_specs=pl.BlockSpec((1, SUBLANES, tn), lambda l, j: (l, 0, j)),
        out_shape=jax.ShapeDtypeStruct((depth, SUBLANES, n_out), F32),
        compiler_params=_cparams("arbitrary", "arbitrary"),
        name="adaln_mod",
    )(c_pad, ada_w, ada_b.reshape(depth, 1, n_out))
    return out[:, :b]


def _norm_kernel(x_ref, g_ref, sc_ref, sh_ref, h_ref, *, tm, rc):
    def body(c, carry):
        r0 = pl.multiple_of(c * rc, rc)
        x = x_ref[pl.ds(r0, rc), :]
        ms = jnp.mean(x * x, axis=-1, keepdims=True)
        y = x * lax.rsqrt(ms + NORM_EPS)
        h = (y * g_ref[...]) * (1.0 + sc_ref[0]) + sh_ref[0]
        h_ref[pl.ds(r0, rc), :] = h.astype(BF16)
        return carry
    lax.fori_loop(0, tm // rc, body, 0)


def _norm_modulate(x2d, gain, scale, shift, *, seq, t):
    n_tok, d_model = x2d.shape
    tm, rc = t["norm_tm"], t["norm_rc"]
    tpb = seq // tm
    return pl.pallas_call(
        functools.partial(_norm_kernel, tm=tm, rc=rc),
        grid=(n_tok // tm,),
        in_specs=[
            pl.BlockSpec((tm, d_model), lambda i: (i, 0)),
            pl.BlockSpec((1, d_model), lambda i: (0, 0)),
            pl.BlockSpec((1, 1, d_model), lambda i: (i // tpb, 0, 0)),
            pl.BlockSpec((1, 1, d_model), lambda i: (i // tpb, 0, 0)),
        ],
        out_specs=pl.BlockSpec((tm, d_model), lambda i: (i, 0)),
        out_shape=jax.ShapeDtypeStruct((n_tok, d_model), BF16),
        compiler_params=_cparams("arbitrary"),
        name="norm_modulate",
    )(x2d, gain, scale, shift)


def _in_kernel(h_ref, w_ref, wf_ref, fb_ref, pw_ref, ps_ref, qg_ref, kg_ref,
               main_ref, cum_ref, tail_sc, fc_sc, *, tm, tpb, gd):
    i = pl.program_id(0)
    j = pl.program_id(1)
    first = (i % tpb) == 0

    def proj():
        return jnp.dot(h_ref[...], w_ref[...], preferred_element_type=F32)

    for g, win in enumerate(POOL_WINDOWS):
        @pl.when(j == g)
        def _pool(g=g, win=win):
            u = proj()
            prev = jnp.where(first, 0.0, tail_sc[g])
            s = jnp.concatenate([prev, u], axis=0)
            shift = 1
            while shift < win:
                s = s + pltpu.roll(s, shift, 0)
                shift *= 2
            wsum = s[POOL_HALO:]
            pos = (i % tpb) * tm + lax.broadcasted_iota(jnp.int32, (tm, 1), 0)
            cnt = jnp.minimum(pos + 1, win).astype(F32)
            mixed = wsum / cnt - u
            po = jnp.dot(mixed.astype(BF16), pw_ref[0], preferred_element_type=F32) * ps_ref[0]
            main_ref[...] = po.astype(BF16)
            tail_sc[g] = u[tm - POOL_HALO:]

    def qk_norm(gain_ref, post_scale):
        r = proj()
        for c in range(gd // HEAD_DIM):
            sl = slice(c * HEAD_DIM, (c + 1) * HEAD_DIM)
            rc_ = r[:, sl]
            ms = jnp.mean(rc_ * rc_, axis=-1, keepdims=True)
            normed = (rc_ * lax.rsqrt(ms + NORM_EPS)) * gain_ref[:, sl]
            if post_scale is not None:
                normed = normed * post_scale
            main_ref[:, sl] = normed.astype(BF16)

    @pl.when((j >= 4) & (j < 8))
    def _q():
        qk_norm(qg_ref, ATT_LOGIT_SCALE)

    @pl.when((j >= 8) & (j < 12))
    def _k():
        qk_norm(kg_ref, None)

    @pl.when((j >= 12) & (j < 16))
    def _v():
        main_ref[...] = proj().astype(BF16)

    @pl.when(j == 16)
    def _forget():
        z = lax.dot_general(wf_ref[...], h_ref[...], (((1,), (1,)), ((), ())),
                            preferred_element_type=F32) + fb_ref[...]
        ls = jnp.minimum(z, 0.0) - jnp.log1p(jnp.exp(-jnp.abs(z)))
        p0 = ls.astype(BF16)
        r1 = ls - p0.astype(F32)
        p1 = r1.astype(BF16)
        p2 = (r1 - p1.astype(F32)).astype(BF16)
        rr = lax.broadcasted_iota(jnp.int32, (tm, tm), 0)
        cc = lax.broadcasted_iota(jnp.int32, (tm, tm), 1)
        tri = jnp.where(rr <= cc, 1.0, 0.0).astype(BF16)
        cum = (jnp.dot(p0, tri, preferred_element_type=F32)
               + jnp.dot(p1, tri, preferred_element_type=F32)
               + jnp.dot(p2, tri, preferred_element_type=F32))
        cum = cum + jnp.where(first, 0.0, fc_sc[:, 0:1])
        cum_ref[0] = cum * LOG2_E
        fc_sc[...] = jnp.broadcast_to(cum[:, tm - 1:tm], fc_sc.shape)


def _in_projection(h2d, w_in_all, layer, n_main, wf_t, fb, pool_w, pool_scale, qg, kg, *, seq, t):
    n_tok, d_model = h2d.shape
    batch = n_tok // seq
    tm = t["in_tm"]
    gd = pool_w.shape[-1]
    hp = wf_t.shape[0]
    tpb = seq // tm
    nj = n_main // gd + 1
    last = n_main // gd - 1
    kern = functools.partial(_in_kernel, tm=tm, tpb=tpb, gd=gd)
    return pl.pallas_call(
        kern,
        grid=(n_tok // tm, nj),
        in_specs=[
            pl.BlockSpec((tm, d_model), lambda i, j: (i, 0)),
            pl.BlockSpec((None, d_model, gd), lambda i, j: (layer, 0, jnp.minimum(j, last))),
            pl.BlockSpec((hp, d_model), lambda i, j: (0, 0)),
            pl.BlockSpec((hp, 1), lambda i, j: (0, 0)),
            pl.BlockSpec((1, gd, gd), lambda i, j: (jnp.minimum(j, 3), 0, 0)),
            pl.BlockSpec((1, 1, gd), lambda i, j: (jnp.minimum(j, 3), 0, 0)),
            pl.BlockSpec((1, gd), lambda i, j: (0, 0)),
            pl.BlockSpec((1, gd), lambda i, j: (0, 0)),
        ],
        out_specs=[
            pl.BlockSpec((tm, gd), lambda i, j: (i, jnp.minimum(j, last))),
            pl.BlockSpec((1, hp, tm), lambda i, j: (i // tpb, 0, i % tpb)),
        ],
        out_shape=[
            jax.ShapeDtypeStruct((n_tok, n_main), BF16),
            jax.ShapeDtypeStruct((batch, hp, seq), F32),
        ],
        scratch_shapes=[
            pltpu.VMEM((len(POOL_WINDOWS), POOL_HALO, gd), F32),
            pltpu.VMEM((hp, LANES), F32),
        ],
        compiler_params=_cparams("arbitrary", "arbitrary"),
        name="in_proj",
    )(h2d, w_in_all, wf_t, fb, pool_w, pool_scale, qg, kg)


def _attn_kernel(q_ref, k_ref, v_ref, c_ref, w1_ref, w2_ref, o_ref, w1o_ref, w2o_ref, *, tq, tk, hps):
    w1o_ref[...] = w1_ref[...].astype(BF16)
    w2o_ref[...] = w2_ref[...].astype(BF16)
    qi = pl.program_id(2)
    heads = [slice(h * HEAD_DIM, (h + 1) * HEAD_DIM) for h in range(hps)]
    kpq = tq // tk

    def block(h, kj, carry, diag_index):
        m_prev, l_prev, acc_prev = carry
        start = pl.multiple_of(kj * tk, tk)
        k = k_ref[pl.ds(start, tk), heads[h]]
        v = v_ref[pl.ds(start, tk), heads[h]]
        s = lax.dot_general(q_ref[:, heads[h]], k, (((1,), (1,)), ((), ())),
                            preferred_element_type=F32)
        s = s - c_ref[h, :, pl.ds(start, tk)]
        if diag_index is not None:
            rr = lax.broadcasted_iota(jnp.int32, (tq, tk), 0)
            cc = lax.broadcasted_iota(jnp.int32, (tq, tk), 1)
            s = jnp.where(cc + diag_index * tk <= rr, s, -jnp.inf)
        m_new = jnp.maximum(m_prev, jnp.max(s, axis=-1, keepdims=True))
        p = jnp.exp2(s - m_new)
        alpha = jnp.exp2(m_prev - m_new)
        l_new = alpha * l_prev + jnp.sum(p, axis=-1, keepdims=True)
        acc_new = alpha * acc_prev + jnp.dot(p.astype(BF16), v, preferred_element_type=F32)
        return m_new, l_new, acc_new

    def full_blocks(g, carry):
        for d in range(kpq):
            carry = tuple(block(h, g * kpq + d, carry[h], None) for h in range(hps))
        return carry

    def diagonal_blocks(_, carry):
        for d in range(kpq):
            carry = tuple(block(h, qi * kpq + d, carry[h], d) for h in range(hps))
        return carry

    init = tuple((jnp.full((tq, 1), -jnp.inf, F32), jnp.zeros((tq, 1), F32),
                  jnp.zeros((tq, HEAD_DIM), F32)) for _ in range(hps))
    carry = lax.fori_loop(0, qi, full_blocks, init)
    carry = lax.fori_loop(0, jnp.minimum(qi + 1, 1), diagonal_blocks, carry)
    for h in range(hps):
        _, l_f, acc_f = carry[h]
        o_ref[:, heads[h]] = (acc_f / l_f).astype(o_ref.dtype)


def _attention(main, cum, w1_f32, w2_f32, layer, *, seq, n_heads, q_off, k_off, v_off, t):
    n_tok = main.shape[0]
    batch = n_tok // seq
    tq, hps = t["att_tq"], min(t["att_heads"], n_heads)
    nq = seq // tq
    hp = cum.shape[1]
    width = hps * HEAD_DIM
    assert n_heads % hps == 0 and hp % hps == 0
    cum3 = cum.reshape(batch * hp, 1, seq)
    kern = functools.partial(_attn_kernel, tq=tq, tk=min(t["att_tk"], tq), hps=hps)
    qb, kb, vb, cb = q_off // width, k_off // width, v_off // width, hp // hps
    n_hg = n_heads // hps
    n_steps = batch * n_hg * nq
    depth, n_exp, d_model, two_f = w1_f32.shape
    fdim = w2_f32.shape[2]
    rows1, rows2 = n_exp * d_model, n_exp * fdim
    assert rows1 % (n_steps * BF16_ROWS) == 0 and rows2 % (n_steps * BF16_ROWS) == 0
    r1, r2 = rows1 // n_steps, rows2 // n_steps

    def step(b, h, i):
        return (b * n_hg + h) * nq + i

    att, w1b, w2b = pl.pallas_call(
        kern,
        grid=(batch, n_hg, nq),
        in_specs=[
            pl.BlockSpec((tq, width), lambda b, h, i: (b * nq + i, qb + h)),
            pl.BlockSpec((seq, width), lambda b, h, i: (b, kb + h)),
            pl.BlockSpec((seq, width), lambda b, h, i: (b, vb + h)),
            pl.BlockSpec((hps, 1, seq), lambda b, h, i: (b * cb + h, 0, 0)),
            pl.BlockSpec((None, r1, two_f), lambda b, h, i: (layer, step(b, h, i), 0)),
            pl.BlockSpec((None, r2, d_model), lambda b, h, i: (layer, step(b, h, i), 0)),
        ],
        out_specs=[
            pl.BlockSpec((tq, width), lambda b, h, i: (b * nq + i, h)),
            pl.BlockSpec((r1, two_f), lambda b, h, i: (step(b, h, i), 0)),
            pl.BlockSpec((r2, d_model), lambda b, h, i: (step(b, h, i), 0)),
        ],
        out_shape=[
            jax.ShapeDtypeStruct((n_tok, n_heads * HEAD_DIM), BF16),
            jax.ShapeDtypeStruct((rows1, two_f), BF16),
            jax.ShapeDtypeStruct((rows2, d_model), BF16),
        ],
        compiler_params=_cparams("arbitrary", "arbitrary", "arbitrary"),
        name="forget_attention",
    )(main, main, main, cum3, w1_f32.reshape(depth, rows1, two_f), w2_f32.reshape(depth, rows2, d_model))
    return att, w1b.reshape(n_exp, d_model, two_f), w2b.reshape(n_exp, fdim, d_model)


def _out_kernel(po_ref, at_ref, wp_ref, wa_ref, x_ref, g_ref, o_ref):
    mix = (jnp.dot(po_ref[...], wp_ref[...], preferred_element_type=F32)
           + jnp.dot(at_ref[...], wa_ref[...], preferred_element_type=F32))
    o_ref[...] = x_ref[...] + g_ref[0] * mix


def _out_projection(main, att, w_out_all, layer, x2d, gate, *, seq, pool_width, t):
    n_tok, d_model = x2d.shape
    tm, tn = t["out_tm"], t["out_tn"]
    att_width = att.shape[1]
    tpb = seq // tm
    return pl.pallas_call(
        _out_kernel,
        grid=(n_tok // tm, d_model // tn),
        in_specs=[
            pl.BlockSpec((tm, pool_width), lambda i, j: (i, 0)),
            pl.BlockSpec((tm, att_width), lambda i, j: (i, 0)),
            pl.BlockSpec((None, pool_width, tn), lambda i, j: (layer, 0, j)),
            pl.BlockSpec((None, att_width, tn), lambda i, j: (layer, pool_width // att_width, j)),
            pl.BlockSpec((tm, tn), lambda i, j: (i, j)),
            pl.BlockSpec((1, 1, tn), lambda i, j: (i // tpb, 0, j)),
        ],
        out_specs=pl.BlockSpec((tm, tn), lambda i, j: (i, j)),
        out_shape=jax.ShapeDtypeStruct((n_tok, d_model), F32),
        compiler_params=_cparams("arbitrary", "arbitrary"),
        name="out_proj_residual",
    )(main, att, w_out_all, w_out_all, x2d, gate)


def _router_kernel(x_ref, g_ref, sc_ref, sh_ref, rw_ref, rb_ref,
                   hp_ref, idx_ref, wts_ref, rank_ref, cnt_ref, h_sc, cnt_sc, *, tm, rc, half):
    i = pl.program_id(0)

    @pl.when(i == 0)
    def _init():
        cnt_sc[...] = jnp.zeros(cnt_sc.shape, F32)

    def body(c, carry):
        r0 = pl.multiple_of(c * rc, rc)
        x = x_ref[pl.ds(r0, rc), :]
        ms = jnp.mean(x * x, axis=-1, keepdims=True)
        y = x * lax.rsqrt(ms + NORM_EPS)
        h = ((y * g_ref[...]) * (1.0 + sc_ref[0]) + sh_ref[0]).astype(BF16)
        h_sc[pl.ds(r0, rc), :] = h
        bits = lax.bitcast_convert_type(h.astype(F32), jnp.uint32)
        lo = lax.shift_right_logical(bits[:, :half], jnp.uint32(16))
        hi = bits[:, half:] & jnp.uint32(0xFFFF0000)
        words = lo | hi
        pitch = half // LANES
        for s in range(pitch):
            hp_ref[pl.ds(r0 * pitch + s, rc, stride=pitch), :] = words[:, s * LANES:(s + 1) * LANES]
        return carry
    lax.fori_loop(0, tm // rc, body, 0)

    logits = jnp.dot(h_sc[...], rw_ref[...], preferred_element_type=F32) + rb_ref[...]
    lane = lax.broadcasted_iota(jnp.int32, logits.shape, 1).astype(F32)
    vals, sels, hots = [], [], []
    cur = logits
    for _ in range(TOP_K):
        mx = jnp.max(cur, axis=-1, keepdims=True)
        sel = jnp.min(jnp.where(cur == mx, lane, float(LANES)), axis=-1, keepdims=True)
        hot = lane == sel
        vals.append(mx)
        sels.append(sel)
        hots.append(hot)
        cur = jnp.where(hot, -jnp.inf, cur)
    exps = [jnp.exp(v - vals[0]) for v in vals]
    denom = exps[0] + exps[1] + exps[2] + exps[3]

    hot_sum = jnp.zeros(logits.shape, F32)
    for hot in hots:
        hot_sum = hot_sum + jnp.where(hot, 1.0, 0.0)
    rr = lax.broadcasted_iota(jnp.int32, (tm, tm), 0)
    cc = lax.broadcasted_iota(jnp.int32, (tm, tm), 1)
    below = jnp.where(cc < rr, 1.0, 0.0).astype(BF16)
    base = jnp.dot(below, hot_sum.astype(BF16), preferred_element_type=F32) + cnt_sc[...]

    idx_out = jnp.zeros(logits.shape, F32)
    wts_out = jnp.zeros(logits.shape, F32)
    rank_out = jnp.zeros(logits.shape, F32)
    for k in range(TOP_K):
        col = lane == float(k)
        rank_k = jnp.sum(jnp.where(hots[k], base, 0.0), axis=-1, keepdims=True)
        idx_out = jnp.where(col, sels[k], idx_out)
        wts_out = jnp.where(col, exps[k] / denom, wts_out)
        rank_out = jnp.where(col, rank_k, rank_out)
    idx_ref[...] = idx_out.astype(jnp.int32)
    wts_ref[...] = wts_out
    rank_ref[...] = rank_out.astype(jnp.int32)
    total = cnt_sc[...] + jnp.sum(hot_sum, axis=0, keepdims=True)
    cnt_sc[...] = total
    cnt_ref[...] = total


def _router(x2d, gain, scale, shift, rw, rb, *, seq, t):
    n_tok, d_model = x2d.shape
    tm, rc = t["rt_tm"], t["norm_rc"]
    tpb = seq // tm
    half = d_model // 2
    kern = functools.partial(_router_kernel, tm=tm, rc=rc, half=half)
    return pl.pallas_call(
        kern,
        grid=(n_tok // tm,),
        in_specs=[
            pl.BlockSpec((tm, d_model), lambda i: (i, 0)),
            pl.BlockSpec((1, d_model), lambda i: (0, 0)),
            pl.BlockSpec((1, 1, d_model), lambda i: (i // tpb, 0, 0)),
            pl.BlockSpec((1, 1, d_model), lambda i: (i // tpb, 0, 0)),
            pl.BlockSpec((d_model, LANES), lambda i: (0, 0)),
            pl.BlockSpec((1, LANES), lambda i: (0, 0)),
        ],
        out_specs=[
            pl.BlockSpec((tm * (half // LANES), LANES), lambda i: (i, 0)),
            pl.BlockSpec((tm, LANES), lambda i: (i, 0)),
            pl.BlockSpec((tm, LANES), lambda i: (i, 0)),
            pl.BlockSpec((tm, LANES), lambda i: (i, 0)),
            pl.BlockSpec((1, LANES), lambda i: (0, 0)),
        ],
        out_shape=[
            jax.ShapeDtypeStruct((n_tok * (half // LANES), LANES), jnp.uint32),
            jax.ShapeDtypeStruct((n_tok, LANES), jnp.int32),
            jax.ShapeDtypeStruct((n_tok, LANES), F32),
            jax.ShapeDtypeStruct((n_tok, LANES), jnp.int32),
            jax.ShapeDtypeStruct((1, LANES), F32),
        ],
        scratch_shapes=[pltpu.VMEM((tm, d_model), BF16), pltpu.VMEM((1, LANES), F32)],
        compiler_params=_cparams("arbitrary"),
        name="norm_router_topk",
    )(x2d, gain, scale, shift, rw, rb)


def _row_token_kernel(dest_ref, zeros_ref, o_ref, sem, *, blk, xp):
    i = pl.program_id(0)

    @pl.when(i == 0)
    def _init():
        fill = pltpu.make_async_copy(zeros_ref, o_ref, sem)
        fill.start()
        fill.wait()

    base = i * blk

    def place(g, carry):
        for u in range(SCALAR_UNROLL):
            a = g * SCALAR_UNROLL + u
            o_ref[dest_ref[a]] = lax.shift_right_logical(base + a, TOP_K.bit_length() - 1) * xp
        return carry
    lax.fori_loop(0, blk // SCALAR_UNROLL, place, 0)


def _row_tokens(dest, n_rows, xp, t):
    n_assign = dest.shape[0]
    blk = min(t["inv_blk"], n_assign)
    assert TOP_K & (TOP_K - 1) == 0 and n_assign % blk == 0 and blk % SCALAR_UNROLL == 0
    return pl.pallas_call(
        functools.partial(_row_token_kernel, blk=blk, xp=xp),
        grid=(n_assign // blk,),
        in_specs=[pl.BlockSpec((blk,), lambda i: (i,), memory_space=pltpu.SMEM),
                  pl.BlockSpec(memory_space=pl.ANY)],
        out_specs=pl.BlockSpec(memory_space=pltpu.SMEM),
        out_shape=jax.ShapeDtypeStruct((n_rows,), jnp.int32),
        scratch_shapes=[pltpu.SemaphoreType.DMA],
        compiler_params=_cparams("arbitrary"),
        name="moe_row_tokens",
    )(dest, jnp.zeros((n_rows,), jnp.int32))


def _expert_kernel(be_ref, na_ref, tok_ref, nxt_ref, src_ref, w1_ref, b1_ref, w2_ref, b2_ref,
                   o_ref, xbuf, sems, *, bm, half, fdim, xp, dp):
    del be_ref
    i = pl.program_id(0)
    n_active = na_ref[0]

    def slab_copy(offs, r, slot):
        src0 = pl.multiple_of(offs[r], xp)
        dst0 = pl.multiple_of(r * dp, SUBLANES)
        return pltpu.make_async_copy(src_ref.at[pl.ds(src0, xp), :],
                                     xbuf.at[slot, pl.ds(dst0, xp), :], sems.at[slot])

    def gather_start(toks, slot):
        def issue(g, carry):
            for u in range(DMA_UNROLL):
                slab_copy(toks, g * DMA_UNROLL + u, slot).start(priority=u % 2)
            return carry
        lax.fori_loop(0, bm // DMA_UNROLL, issue, 0)

    def gather_wait(toks, slot):
        def drain(g, carry):
            for u in range(DMA_UNROLL):
                slab_copy(toks, g * DMA_UNROLL + u, slot).wait()
            return carry
        lax.fori_loop(0, bm // DMA_UNROLL, drain, 0)

    def compute(slot):
        words = jnp.concatenate([xbuf[slot, pl.ds(s, bm, stride=dp), :] for s in range(xp)], axis=1)
        lo = lax.bitcast_convert_type(lax.shift_left(words, jnp.uint32(16)), F32).astype(BF16)
        hi = lax.bitcast_convert_type(words & jnp.uint32(0xFFFF0000), F32).astype(BF16)
        hid = (jnp.dot(lo, w1_ref[0, :half, :], preferred_element_type=F32)
               + jnp.dot(hi, w1_ref[0, half:, :], preferred_element_type=F32)) + b1_ref[0]
        glu = jnp.minimum(hid[:, :fdim], SWIGLU_LIMIT)
        lin = jnp.clip(hid[:, fdim:], -SWIGLU_LIMIT, SWIGLU_LIMIT)
        act = (glu / (1.0 + jnp.exp(-SWIGLU_ALPHA * glu))) * (lin + 1.0)
        y = jnp.dot(act.astype(BF16), w2_ref[0], preferred_element_type=F32) + b2_ref[0]
        bits = lax.bitcast_convert_type(y.astype(BF16).astype(F32), jnp.uint32)
        out_words = lax.shift_right_logical(bits[:, :half], jnp.uint32(16)) | (
            bits[:, half:] & jnp.uint32(0xFFFF0000))
        for c in range(xp):
            o_ref[pl.ds(c, bm, stride=xp), :] = out_words[:, c * LANES:(c + 1) * LANES]

    @pl.when(i == 0)
    def _first():
        gather_start(tok_ref, 0)

    for slot in range(2):
        @pl.when((i % 2 == slot) & (i + 1 < n_active))
        def _prefetch(slot=slot):
            gather_start(nxt_ref, 1 - slot)

        @pl.when((i % 2 == slot) & (i < n_active))
        def _active(slot=slot):
            gather_wait(tok_ref, slot)
            compute(slot)

    @pl.when(i >= n_active)
    def _unused():
        o_ref[...] = jnp.zeros(o_ref.shape, o_ref.dtype)


def _vmem_pitch(rows):
    groups = -(-rows // SUBLANES)
    return (groups + 1 - groups % 2) * SUBLANES


def _experts(block_e, n_active, row_tok, packed, w1, b1, w2, b2, *, bm, xp):
    dp = _vmem_pitch(xp)
    n_blocks = row_tok.shape[0] // bm
    n_exp, d_model, two_f = w1.shape
    fdim, half = two_f // 2, d_model // 2
    grid_spec = pltpu.PrefetchScalarGridSpec(
        num_scalar_prefetch=2,
        grid=(n_blocks,),
        in_specs=[
            pl.BlockSpec((bm,), lambda i, be, na: (i,), memory_space=pltpu.SMEM),
            pl.BlockSpec((bm,), lambda i, be, na: (jnp.minimum(i + 1, n_blocks - 1),),
                         memory_space=pltpu.SMEM),
            pl.BlockSpec(memory_space=pl.ANY),
            pl.BlockSpec((1, d_model, two_f), lambda i, be, na: (be[i], 0, 0)),
            pl.BlockSpec((1, 1, two_f), lambda i, be, na: (be[i], 0, 0)),
            pl.BlockSpec((1, fdim, d_model), lambda i, be, na: (be[i], 0, 0)),
            pl.BlockSpec((1, 1, d_model), lambda i, be, na: (be[i], 0, 0)),
        ],
        out_specs=pl.BlockSpec((bm * xp, LANES), lambda i, be, na: (i, 0)),
        scratch_shapes=[pltpu.VMEM((2, bm * dp, LANES), packed.dtype),
                        pltpu.SemaphoreType.DMA((2,))],
    )
    return pl.pallas_call(
        functools.partial(_expert_kernel, bm=bm, half=half, fdim=fdim, xp=xp, dp=dp),
        grid_spec=grid_spec,
        out_shape=jax.ShapeDtypeStruct((n_blocks * bm * xp, LANES), packed.dtype),
        compiler_params=_cparams("arbitrary"),
        name="moe_experts",
    )(block_e, n_active, row_tok, row_tok, packed, w1, b1.reshape(n_exp, 1, two_f),
      w2, b2.reshape(n_exp, 1, d_model))


def _combine_kernel(dest_ref, nxt_ref, y_ref, wts_ref, x_ref, g_ref, *rest, tm, yp, dp, half, n_steps,
                    next_norm):
    if next_norm:
        ng_ref, nsc_ref, nsh_ref, o_ref, hn_ref, buf, sems = rest
    else:
        o_ref, buf, sems = rest
    i = pl.program_id(0)

    def slab_copy(offs, r, k, slot):
        src0 = pl.multiple_of(offs[r * TOP_K + k], yp)
        dst0 = pl.multiple_of(r * dp, SUBLANES)
        return pltpu.make_async_copy(y_ref.at[pl.ds(src0, yp), :],
                                     buf.at[slot, k, pl.ds(dst0, yp), :], sems.at[slot])

    def gather_start(dests, slot):
        def issue(g, carry):
            for u in range(DMA_UNROLL // TOP_K):
                for k in range(TOP_K):
                    slab_copy(dests, g * (DMA_UNROLL // TOP_K) + u, k, slot).start(priority=k % 2)
            return carry
        lax.fori_loop(0, tm * TOP_K // DMA_UNROLL, issue, 0)

    def gather_wait(dests, slot):
        def drain(g, carry):
            for u in range(DMA_UNROLL // TOP_K):
                for k in range(TOP_K):
                    slab_copy(dests, g * (DMA_UNROLL // TOP_K) + u, k, slot).wait()
            return carry
        lax.fori_loop(0, tm * TOP_K // DMA_UNROLL, drain, 0)

    def compute(slot):
        w = wts_ref[...]
        wk = [jnp.broadcast_to(w[:, k:k + 1], (tm, LANES)) for k in range(TOP_K)]
        for c in range(yp):
            acc_lo = acc_hi = None
            for k in range(TOP_K):
                words = buf[slot, k, pl.ds(c, tm, stride=dp), :]
                lo = wk[k] * lax.bitcast_convert_type(lax.shift_left(words, jnp.uint32(16)), F32)
                hi = wk[k] * lax.bitcast_convert_type(words & jnp.uint32(0xFFFF0000), F32)
                acc_lo = lo if acc_lo is None else acc_lo + lo
                acc_hi = hi if acc_hi is None else acc_hi + hi
            for base, acc in ((0, acc_lo), (half, acc_hi)):
                sl = slice(base + c * LANES, base + (c + 1) * LANES)
                o_ref[:, sl] = x_ref[:, sl] + g_ref[0, :, sl] * acc
        if next_norm:
            x_new = o_ref[...]
            ms = jnp.mean(x_new * x_new, axis=-1, keepdims=True)
            y = x_new * lax.rsqrt(ms + NORM_EPS)
            hn_ref[...] = ((y * ng_ref[...]) * (1.0 + nsc_ref[0]) + nsh_ref[0]).astype(BF16)

    @pl.when(i == 0)
    def _first():
        gather_start(dest_ref, 0)

    for slot in range(2):
        @pl.when((i % 2 == slot) & (i + 1 < n_steps))
        def _prefetch(slot=slot):
            gather_start(nxt_ref, 1 - slot)

        @pl.when(i % 2 == slot)
        def _active(slot=slot):
            gather_wait(dest_ref, slot)
            compute(slot)


def _combine(dest_off, ys, yp, wts, x2d, gate, next_norm, *, seq, t):
    n_tok, d_model = x2d.shape
    tm = t["cmb_tm"]
    dp = _vmem_pitch(yp)
    tpb = seq // tm
    n_steps = n_tok // tm
    row_spec = pl.BlockSpec((tm, d_model), lambda i: (i, 0))
    mod_spec = pl.BlockSpec((1, 1, d_model), lambda i: (i // tpb, 0, 0))
    in_specs = [
        pl.BlockSpec((tm * TOP_K,), lambda i: (i,), memory_space=pltpu.SMEM),
        pl.BlockSpec((tm * TOP_K,), lambda i: (jnp.minimum(i + 1, n_steps - 1),),
                     memory_space=pltpu.SMEM),
        pl.BlockSpec(memory_space=pl.ANY),
        pl.BlockSpec((tm, LANES), lambda i: (i, 0)),
        row_spec,
        mod_spec,
    ]
    out_specs, out_shape = [row_spec], [jax.ShapeDtypeStruct((n_tok, d_model), F32)]
    args = [dest_off, dest_off, ys, wts, x2d, gate]
    if next_norm is not None:
        in_specs += [pl.BlockSpec((1, d_model), lambda i: (0, 0)), mod_spec, mod_spec]
        out_specs.append(row_spec)
        out_shape.append(jax.ShapeDtypeStruct((n_tok, d_model), BF16))
        args += list(next_norm)
    outs = pl.pallas_call(
        functools.partial(_combine_kernel, tm=tm, yp=yp, dp=dp, half=d_model // 2, n_steps=n_steps,
                          next_norm=next_norm is not None),
        grid=(n_steps,),
        in_specs=in_specs,
        out_specs=out_specs,
        out_shape=out_shape,
        scratch_shapes=[pltpu.VMEM((2, TOP_K, tm * dp, LANES), ys.dtype), pltpu.SemaphoreType.DMA((2,))],
        compiler_params=_cparams("arbitrary"),
        name="moe_combine_residual",
    )(*args)
    return (outs[0], outs[1]) if next_norm is not None else (outs[0], None)


def _layer(layer, x2d, h1, mod, next_norm, w_in_all, forget_b, pool_w, pool_scale, q_norm_g,
           k_norm_g, w_out_all, norm2_g, router_w, router_b, w1_f32, b1, w2_f32, b2, *, seq, t):
    n_tok, d_model = x2d.shape
    batch = n_tok // seq
    n_groups, gd, _ = pool_w.shape
    pool_width = n_groups * gd
    att_width = d_model - pool_width
    n_heads = att_width // HEAD_DIM
    n_exp = router_w.shape[1]
    assert pool_width == att_width and n_groups == len(POOL_WINDOWS) and gd % HEAD_DIM == 0
    n_main = pool_width + 3 * att_width
    hp = -(-n_heads // BF16_ROWS) * BF16_ROWS

    shift1, scale1, gate1, shift2, scale2, gate2 = [
        m.reshape(batch, 1, d_model) for m in jnp.split(mod, N_MOD, axis=-1)]

    wf_t = jnp.zeros((hp, d_model), BF16).at[:n_heads].set(w_in_all[layer, :, n_main:].T)
    fb = jnp.zeros((hp, 1), F32).at[:n_heads, 0].set(forget_b)
    reps = gd // HEAD_DIM
    main, cum = _in_projection(
        h1, w_in_all, layer, n_main, wf_t, fb,
        pool_w.astype(BF16), pool_scale.reshape(n_groups, 1, gd),
        jnp.tile(q_norm_g, reps).reshape(1, gd), jnp.tile(k_norm_g, reps).reshape(1, gd),
        seq=seq, t=t)
    att, w1, w2 = _attention(main, cum, w1_f32, w2_f32, layer, seq=seq, n_heads=n_heads,
                             q_off=pool_width, k_off=pool_width + att_width,
                             v_off=pool_width + 2 * att_width, t=t)
    x1 = _out_projection(main, att, w_out_all, layer, x2d, gate1, seq=seq,
                         pool_width=pool_width, t=t)

    rw = jnp.zeros((d_model, LANES), BF16).at[:, :n_exp].set(router_w.astype(BF16))
    rb = jnp.full((1, LANES), -jnp.inf, F32).at[0, :n_exp].set(router_b)
    packed, idx, wts, rank, counts = _router(
        x1, norm2_g.reshape(1, d_model), scale2, shift2, rw, rb, seq=seq, t=t)

    bm = t["moe_bm"]
    n_assign = n_tok * TOP_K
    n_blocks = -(-n_assign // bm) + n_exp
    counts = counts[0, :n_exp].astype(jnp.int32)
    padded = (counts + bm - 1) // bm * bm
    pad_end = jnp.cumsum(padded)
    pad_start = pad_end - padded
    dest = (pad_start[idx[:, :TOP_K]] + rank[:, :TOP_K]).reshape(-1)
    xp = d_model // 2 // LANES
    row_tok = _row_tokens(dest, n_blocks * bm, xp, t)
    block_start = jnp.arange(n_blocks, dtype=jnp.int32) * bm
    block_e = jnp.minimum(jnp.sum(block_start[:, None] >= pad_end[None, :], axis=1),
                          n_exp - 1).astype(jnp.int32)
    n_active = (pad_end[-1:] // bm).astype(jnp.int32)

    ys = _experts(block_e, n_active, row_tok, packed, w1, b1, w2, b2, bm=bm, xp=xp)
    return _combine(dest * xp, ys, xp, wts, x1, gate2, next_norm, seq=seq, t=t)


def kernel(x, c, ada_w, ada_b, norm1_g, w_in, forget_b, pool_w, pool_scale, q_norm_g, k_norm_g,
           w_out, norm2_g, router_w, router_b, expert_w1, expert_b1, expert_w2, expert_b2):
    batch, seq, d_model = x.shape
    depth = ada_w.shape[0]
    t = _tiles(batch * seq, seq, d_model)
    mod = _modulation(c, ada_w, ada_b, t["mod_tn"])
    x2d = x.reshape(batch * seq, d_model)
    w_in_all, w_out_all = _cast_bf16(w_in, t["cast_rows"]), _cast_bf16(w_out, t["cast_rows"])

    def first_norm(l):
        shift1, scale1 = jnp.split(mod[l], N_MOD, axis=-1)[:2]
        return (norm1_g[l].reshape(1, d_model), scale1.reshape(batch, 1, d_model),
                shift1.reshape(batch, 1, d_model))

    h1 = _norm_modulate(x2d, *first_norm(0), seq=seq, t=t)
    for l in range(depth):
        next_norm = first_norm(l + 1) if l + 1 < depth else None
        x2d, h1 = _layer(l, x2d, h1, mod[l], next_norm, w_in_all, forget_b[l], pool_w[l],
                         pool_scale[l], q_norm_g[l], k_norm_g[l], w_out_all, norm2_g[l], router_w[l],
                         router_b[l], expert_w1, expert_b1[l], expert_w2, expert_b2[l], seq=seq, t=t)
    return x2d.reshape(batch, seq, d_model)
```

```python
import functools

import jax
import jax.numpy as jnp
from jax import lax
from jax.experimental import pallas as pl
from jax.experimental.pallas import tpu as pltpu

HEAD_DIM = 128
POOL_WINDOWS = (2, 4, 8, 16)
TOP_K = 4
N_MOD = 6
NORM_EPS = 1e-6
SWIGLU_ALPHA = 1.702
SWIGLU_LIMIT = 7.0
LOG2_E = 1.4426950408889634
ATT_LOGIT_SCALE = HEAD_DIM ** -0.5 * LOG2_E

LANES = 128
SUBLANES = 8
BF16_ROWS = 16
SCALAR_UNROLL = 8
DMA_UNROLL = 32
POOL_HALO = 16
VMEM_LIMIT_BYTES = 56 * 2**20

F32 = jnp.float32
BF16 = jnp.bfloat16


def _cparams(*sem):
    return pltpu.CompilerParams(dimension_semantics=sem, vmem_limit_bytes=VMEM_LIMIT_BYTES)


def _tiles(n_tok, seq, d_model):
    return dict(
        mod_tn=min(512, d_model),
        cast_rows=min(256, d_model),
        norm_tm=min(512, seq),
        in_tm=min(1024, seq),
        norm_rc=min(64, seq),
        att_tq=min(1024, seq),
        att_tk=512,
        att_heads=2,
        out_tm=min(1024, seq),
        out_tn=min(512, d_model),
        rt_tm=min(512, seq),
        moe_bm=min(256, n_tok),
        inv_blk=8192,
        cmb_tm=min(128, seq),
    )


def _cast_kernel(x_ref, o_ref):
    o_ref[...] = x_ref[...].astype(BF16)


def _cast_bf16(w, rows):
    depth, n_rows, n_cols = w.shape
    spec = pl.BlockSpec((1, rows, n_cols), lambda l, i: (l, i, 0))
    return pl.pallas_call(
        _cast_kernel,
        grid=(depth, n_rows // rows),
        in_specs=[spec],
        out_specs=spec,
        out_shape=jax.ShapeDtypeStruct(w.shape, BF16),
        compiler_params=_cparams("arbitrary", "arbitrary"),
        name="weight_cast",
    )(w)


def _mod_kernel(c_ref, w_ref, b_ref, o_ref):
    c = c_ref[...]
    ca = c / (1.0 + jnp.exp(-c))
    w = w_ref[0].astype(BF16)
    o_ref[0] = jnp.dot(ca.astype(BF16), w, preferred_element_type=F32) + b_ref[0]


def _modulation(c, ada_w, ada_b, tn):
    depth, d_model, n_out = ada_w.shape
    b = c.shape[0]
    c_pad = jnp.zeros((SUBLANES, d_model), F32).at[:b].set(c)
    out = pl.pallas_call(
        _mod_kernel,
        grid=(depth, n_out // tn),
        in_specs=[
            pl.BlockSpec((SUBLANES, d_model), lambda l, j: (0, 0)),
            pl.BlockSpec((1, d_model, tn), lambda l, j: (l, 0, j)),
            pl.BlockSpec((1, 1, tn), lambda l, j: (l, 0, j)),
        ],
        out_specs=pl.BlockSpec((1, SUBLANES, tn), lambda l, j: (l, 0, j)),
        out_shape=jax.ShapeDtypeStruct((depth, SUBLANES, n_out), F32),
        compiler_params=_cparams("arbitrary", "arbitrary"),
        name="adaln_mod",
    )(c_pad, ada_w, ada_b.reshape(depth, 1, n_out))
    return out[:, :b]


def _norm_kernel(x_ref, g_ref, sc_ref, sh_ref, h_ref, *, tm, rc):
    def body(c, carry):
        r0 = pl.multiple_of(c * rc, rc)
        x = x_ref[pl.ds(r0, rc), :]
        ms = jnp.mean(x * x, axis=-1, keepdims=True)
        y = x * lax.rsqrt(ms + NORM_EPS)
        h = (y * g_ref[...]) * (1.0 + sc_ref[0]) + sh_ref[0]
        h_ref[pl.ds(r0, rc), :] = h.astype(BF16)
        return carry
    lax.fori_loop(0, tm // rc, body, 0)


def _norm_modulate(x2d, gain, scale, shift, *, seq, t):
    n_tok, d_model = x2d.shape
    tm, rc = t["norm_tm"], t["norm_rc"]
    tpb = seq // tm
    return pl.pallas_call(
        functools.partial(_norm_kernel, tm=tm, rc=rc),
        grid=(n_tok // tm,),
        in_specs=[
            pl.BlockSpec((tm, d_model), lambda i: (i, 0)),
            pl.BlockSpec((1, d_model), lambda i: (0, 0)),
            pl.BlockSpec((1, 1, d_model), lambda i: (i // tpb, 0, 0)),
            pl.BlockSpec((1, 1, d_model), lambda i: (i // tpb, 0, 0)),
        ],
        out_specs=pl.BlockSpec((tm, d_model), lambda i: (i, 0)),
        out_shape=jax.ShapeDtypeStruct((n_tok, d_model), BF16),
        compiler_params=_cparams("arbitrary"),
        name="norm_modulate",
    )(x2d, gain, scale, shift)


def _in_kernel(h_ref, w_ref, wf_ref, fb_ref, pw_ref, ps_ref, qg_ref, kg_ref,
               main_ref, cum_ref, tail_sc, fc_sc, *, tm, tpb, gd):
    i = pl.program_id(0)
    j = pl.program_id(1)
    first = (i % tpb) == 0

    def proj():
        return lax.dot_general(h_ref[...], w_ref[...], (((1,), (1,)), ((), ())),
                               preferred_element_type=F32)

    for g, win in enumerate(POOL_WINDOWS):
        @pl.when(j == g)
        def _pool(g=g, win=win):
            u = proj()
            prev = jnp.where(first, 0.0, tail_sc[g])
            s = jnp.concatenate([prev, u], axis=0)
            shift = 1
            while shift < win:
                s = s + pltpu.roll(s, shift, 0)
                shift *= 2
            wsum = s[POOL_HALO:]
            pos = (i % tpb) * tm + lax.broadcasted_iota(jnp.int32, (tm, 1), 0)
            cnt = jnp.minimum(pos + 1, win).astype(F32)
            mixed = wsum / cnt - u
            po = jnp.dot(mixed.astype(BF16), pw_ref[0], preferred_element_type=F32) * ps_ref[0]
            main_ref[...] = po.astype(BF16)
            tail_sc[g] = u[tm - POOL_HALO:]

    def qk_norm(gain_ref, post_scale):
        r = proj()
        for c in range(gd // HEAD_DIM):
            sl = slice(c * HEAD_DIM, (c + 1) * HEAD_DIM)
            rc_ = r[:, sl]
            ms = jnp.mean(rc_ * rc_, axis=-1, keepdims=True)
            normed = (rc_ * lax.rsqrt(ms + NORM_EPS)) * gain_ref[:, sl]
            if post_scale is not None:
                normed = normed * post_scale
            main_ref[:, sl] = normed.astype(BF16)

    @pl.when((j >= 4) & (j < 8))
    def _q():
        qk_norm(qg_ref, ATT_LOGIT_SCALE)

    @pl.when((j >= 8) & (j < 12))
    def _k():
        qk_norm(kg_ref, None)

    @pl.when((j >= 12) & (j < 16))
    def _v():
        main_ref[...] = proj().astype(BF16)

    @pl.when(j == 16)
    def _forget():
        z = lax.dot_general(wf_ref[...], h_ref[...], (((1,), (1,)), ((), ())),
                            preferred_element_type=F32) + fb_ref[...]
        ls = jnp.minimum(z, 0.0) - jnp.log1p(jnp.exp(-jnp.abs(z)))
        p0 = ls.astype(BF16)
        r1 = ls - p0.astype(F32)
        p1 = r1.astype(BF16)
        p2 = (r1 - p1.astype(F32)).astype(BF16)
        rr = lax.broadcasted_iota(jnp.int32, (tm, tm), 0)
        cc = lax.broadcasted_iota(jnp.int32, (tm, tm), 1)
        tri = jnp.where(rr <= cc, 1.0, 0.0).astype(BF16)
        cum = (jnp.dot(p0, tri, preferred_element_type=F32)
               + jnp.dot(p1, tri, preferred_element_type=F32)
               + jnp.dot(p2, tri, preferred_element_type=F32))
        cum = cum + jnp.where(first, 0.0, fc_sc[:, 0:1])
        cum_ref[0] = cum * LOG2_E
        fc_sc[...] = jnp.broadcast_to(cum[:, tm - 1:tm], fc_sc.shape)


def _in_projection(h2d, w_in_all, layer, n_main, wf_t, fb, pool_w, pool_scale, qg, kg, *, seq, t):
    n_tok, d_model = h2d.shape
    batch = n_tok // seq
    tm = t["in_tm"]
    gd = pool_w.shape[-1]
    hp = wf_t.shape[0]
    tpb = seq // tm
    nj = n_main // gd + 1
    last = n_main // gd - 1
    kern = functools.partial(_in_kernel, tm=tm, tpb=tpb, gd=gd)
    return pl.pallas_call(
        kern,
        grid=(n_tok // tm, nj),
        in_specs=[
            pl.BlockSpec((tm, d_model), lambda i, j: (i, 0)),
            pl.BlockSpec((None, gd, d_model), lambda i, j: (layer, jnp.minimum(j, last), 0)),
            pl.BlockSpec((hp, d_model), lambda i, j: (0, 0)),
            pl.BlockSpec((hp, 1), lambda i, j: (0, 0)),
            pl.BlockSpec((1, gd, gd), lambda i, j: (jnp.minimum(j, 3), 0, 0)),
            pl.BlockSpec((1, 1, gd), lambda i, j: (jnp.minimum(j, 3), 0, 0)),
            pl.BlockSpec((1, gd), lambda i, j: (0, 0)),
            pl.BlockSpec((1, gd), lambda i, j: (0, 0)),
        ],
        out_specs=[
            pl.BlockSpec((tm, gd), lambda i, j: (i, jnp.minimum(j, last))),
            pl.BlockSpec((1, hp, tm), lambda i, j: (i // tpb, 0, i % tpb)),
        ],
        out_shape=[
            jax.ShapeDtypeStruct((n_tok, n_main), BF16),
            jax.ShapeDtypeStruct((batch, hp, seq), F32),
        ],
        scratch_shapes=[
            pltpu.VMEM((len(POOL_WINDOWS), POOL_HALO, gd), F32),
            pltpu.VMEM((hp, LANES), F32),
        ],
        compiler_params=_cparams("arbitrary", "arbitrary"),
        name="in_proj",
    )(h2d, w_in_all, wf_t, fb, pool_w, pool_scale, qg, kg)


def _attn_kernel(q_ref, k_ref, v_ref, c_ref, w1_ref, w2_ref, o_ref, w1o_ref, w2o_ref, *, tq, tk, hps):
    w1o_ref[...] = w1_ref[...].astype(BF16)
    w2o_ref[...] = w2_ref[...].astype(BF16)
    qi = pl.program_id(2)
    heads = [slice(h * HEAD_DIM, (h + 1) * HEAD_DIM) for h in range(hps)]
    kpq = tq // tk

    def block(h, kj, carry, diag_index):
        m_prev, l_prev, acc_prev = carry
        start = pl.multiple_of(kj * tk, tk)
        k = k_ref[pl.ds(start, tk), heads[h]]
        v = v_ref[pl.ds(start, tk), heads[h]]
        s = lax.dot_general(q_ref[:, heads[h]], k, (((1,), (1,)), ((), ())),
                            preferred_element_type=F32)
        s = s - c_ref[h, :, pl.ds(start, tk)]
        if diag_index is not None:
            rr = lax.broadcasted_iota(jnp.int32, (tq, tk), 0)
            cc = lax.broadcasted_iota(jnp.int32, (tq, tk), 1)
            s = jnp.where(cc + diag_index * tk <= rr, s, -jnp.inf)
        m_new = jnp.maximum(m_prev, jnp.max(s, axis=-1, keepdims=True))
        p = jnp.exp2(s - m_new)
        alpha = jnp.exp2(m_prev - m_new)
        l_new = alpha * l_prev + jnp.sum(p, axis=-1, keepdims=True)
        acc_new = alpha * acc_prev + jnp.dot(p.astype(BF16), v, preferred_element_type=F32)
        return m_new, l_new, acc_new

    def full_blocks(g, carry):
        for d in range(kpq):
            carry = tuple(block(h, g * kpq + d, carry[h], None) for h in range(hps))
        return carry

    def diagonal_blocks(_, carry):
        for d in range(kpq):
            carry = tuple(block(h, qi * kpq + d, carry[h], d) for h in range(hps))
        return carry

    init = tuple((jnp.full((tq, 1), -jnp.inf, F32), jnp.zeros((tq, 1), F32),
                  jnp.zeros((tq, HEAD_DIM), F32)) for _ in range(hps))
    carry = lax.fori_loop(0, qi, full_blocks, init)
    carry = lax.fori_loop(0, jnp.minimum(qi + 1, 1), diagonal_blocks, carry)
    for h in range(hps):
        _, l_f, acc_f = carry[h]
        o_ref[:, heads[h]] = (acc_f / l_f).astype(o_ref.dtype)


def _attention(main, cum, w1_f32, w2_f32, layer, *, seq, n_heads, q_off, k_off, v_off, t):
    n_tok = main.shape[0]
    batch = n_tok // seq
    tq, hps = t["att_tq"], min(t["att_heads"], n_heads)
    nq = seq // tq
    hp = cum.shape[1]
    width = hps * HEAD_DIM
    assert n_heads % hps == 0 and hp % hps == 0
    cum3 = cum.reshape(batch * hp, 1, seq)
    kern = functools.partial(_attn_kernel, tq=tq, tk=min(t["att_tk"], tq), hps=hps)
    qb, kb, vb, cb = q_off // width, k_off // width, v_off // width, hp // hps
    n_hg = n_heads // hps
    n_steps = batch * n_hg * nq
    depth, n_exp, d_model, two_f = w1_f32.shape
    fdim = w2_f32.shape[2]
    rows1, rows2 = n_exp * d_model, n_exp * fdim
    assert rows1 % (n_steps * BF16_ROWS) == 0 and rows2 % (n_steps * BF16_ROWS) == 0
    r1, r2 = rows1 // n_steps, rows2 // n_steps

    def step(b, h, i):
        return (b * n_hg + h) * nq + i

    att, w1b, w2b = pl.pallas_call(
        kern,
        grid=(batch, n_hg, nq),
        in_specs=[
            pl.BlockSpec((tq, width), lambda b, h, i: (b * nq + i, qb + h)),
            pl.BlockSpec((seq, width), lambda b, h, i: (b, kb + h)),
            pl.BlockSpec((seq, width), lambda b, h, i: (b, vb + h)),
            pl.BlockSpec((hps, 1, seq), lambda b, h, i: (b * cb + h, 0, 0)),
            pl.BlockSpec((None, r1, two_f), lambda b, h, i: (layer, step(b, h, i), 0)),
            pl.BlockSpec((None, r2, d_model), lambda b, h, i: (layer, step(b, h, i), 0)),
        ],
        out_specs=[
            pl.BlockSpec((tq, width), lambda b, h, i: (b * nq + i, h)),
            pl.BlockSpec((r1, two_f), lambda b, h, i: (step(b, h, i), 0)),
            pl.BlockSpec((r2, d_model), lambda b, h, i: (step(b, h, i), 0)),
        ],
        out_shape=[
            jax.ShapeDtypeStruct((n_tok, n_heads * HEAD_DIM), BF16),
            jax.ShapeDtypeStruct((rows1, two_f), BF16),
            jax.ShapeDtypeStruct((rows2, d_model), BF16),
        ],
        compiler_params=_cparams("arbitrary", "arbitrary", "arbitrary"),
        name="forget_attention",
    )(main, main, main, cum3, w1_f32.reshape(depth, rows1, two_f), w2_f32.reshape(depth, rows2, d_model))
    return att, w1b.reshape(n_exp, d_model, two_f), w2b.reshape(n_exp, fdim, d_model)


def _out_kernel(po_ref, at_ref, wp_ref, wa_ref, x_ref, g_ref, o_ref):
    mix = (jnp.dot(po_ref[...], wp_ref[...], preferred_element_type=F32)
           + jnp.dot(at_ref[...], wa_ref[...], preferred_element_type=F32))
    o_ref[...] = x_ref[...] + g_ref[0] * mix


def _out_projection(main, att, w_out_all, layer, x2d, gate, *, seq, pool_width, t):
    n_tok, d_model = x2d.shape
    tm, tn = t["out_tm"], t["out_tn"]
    att_width = att.shape[1]
    tpb = seq // tm
    return pl.pallas_call(
        _out_kernel,
        grid=(n_tok // tm, d_model // tn),
        in_specs=[
            pl.BlockSpec((tm, pool_width), lambda i, j: (i, 0)),
            pl.BlockSpec((tm, att_width), lambda i, j: (i, 0)),
            pl.BlockSpec((None, pool_width, tn), lambda i, j: (layer, 0, j)),
            pl.BlockSpec((None, att_width, tn), lambda i, j: (layer, pool_width // att_width, j)),
            pl.BlockSpec((tm, tn), lambda i, j: (i, j)),
            pl.BlockSpec((1, 1, tn), lambda i, j: (i // tpb, 0, j)),
        ],
        out_specs=pl.BlockSpec((tm, tn), lambda i, j: (i, j)),
        out_shape=jax.ShapeDtypeStruct((n_tok, d_model), F32),
        compiler_params=_cparams("arbitrary", "arbitrary"),
        name="out_proj_residual",
    )(main, att, w_out_all, w_out_all, x2d, gate)


def _router_kernel(x_ref, g_ref, sc_ref, sh_ref, rw_ref, rb_ref,
                   hp_ref, idx_ref, wts_ref, rank_ref, cnt_ref, h_sc, cnt_sc, *, tm, rc, half):
    i = pl.program_id(0)

    @pl.when(i == 0)
    def _init():
        cnt_sc[...] = jnp.zeros(cnt_sc.shape, F32)

    def body(c, carry):
        r0 = pl.multiple_of(c * rc, rc)
        x = x_ref[pl.ds(r0, rc), :]
        ms = jnp.mean(x * x, axis=-1, keepdims=True)
        y = x * lax.rsqrt(ms + NORM_EPS)
        h = ((y * g_ref[...]) * (1.0 + sc_ref[0]) + sh_ref[0]).astype(BF16)
        h_sc[pl.ds(r0, rc), :] = h
        bits = lax.bitcast_convert_type(h.astype(F32), jnp.uint32)
        lo = lax.shift_right_logical(bits[:, :half], jnp.uint32(16))
        hi = bits[:, half:] & jnp.uint32(0xFFFF0000)
        words = lo | hi
        pitch = half // LANES
        for s in range(pitch):
            hp_ref[pl.ds(r0 * pitch + s, rc, stride=pitch), :] = words[:, s * LANES:(s + 1) * LANES]
        return carry
    lax.fori_loop(0, tm // rc, body, 0)

    logits = jnp.dot(h_sc[...], rw_ref[...], preferred_element_type=F32) + rb_ref[...]
    lane = lax.broadcasted_iota(jnp.int32, logits.shape, 1).astype(F32)
    vals, sels, hots = [], [], []
    cur = logits
    for _ in range(TOP_K):
        mx = jnp.max(cur, axis=-1, keepdims=True)
        sel = jnp.min(jnp.where(cur == mx, lane, float(LANES)), axis=-1, keepdims=True)
        hot = lane == sel
        vals.append(mx)
        sels.append(sel)
        hots.append(hot)
        cur = jnp.where(hot, -jnp.inf, cur)
    exps = [jnp.exp(v - vals[0]) for v in vals]
    denom = exps[0] + exps[1] + exps[2] + exps[3]

    hot_sum = jnp.zeros(logits.shape, F32)
    for hot in hots:
        hot_sum = hot_sum + jnp.where(hot, 1.0, 0.0)
    rr = lax.broadcasted_iota(jnp.int32, (tm, tm), 0)
    cc = lax.broadcasted_iota(jnp.int32, (tm, tm), 1)
    below = jnp.where(cc < rr, 1.0, 0.0).astype(BF16)
    base = jnp.dot(below, hot_sum.astype(BF16), preferred_element_type=F32) + cnt_sc[...]

    idx_out = jnp.zeros(logits.shape, F32)
    wts_out = jnp.zeros(logits.shape, F32)
    rank_out = jnp.zeros(logits.shape, F32)
    for k in range(TOP_K):
        col = lane == float(k)
        rank_k = jnp.sum(jnp.where(hots[k], base, 0.0), axis=-1, keepdims=True)
        idx_out = jnp.where(col, sels[k], idx_out)
        wts_out = jnp.where(col, exps[k] / denom, wts_out)
        rank_out = jnp.where(col, rank_k, rank_out)
    idx_ref[...] = idx_out.astype(jnp.int32)
    wts_ref[...] = wts_out
    rank_ref[...] = rank_out.astype(jnp.int32)
    total = cnt_sc[...] + jnp.sum(hot_sum, axis=0, keepdims=True)
    cnt_sc[...] = total
    cnt_ref[...] = total


def _router(x2d, gain, scale, shift, rw, rb, *, seq, t):
    n_tok, d_model = x2d.shape
    tm, rc = t["rt_tm"], t["norm_rc"]
    tpb = seq // tm
    half = d_model // 2
    kern = functools.partial(_router_kernel, tm=tm, rc=rc, half=half)
    return pl.pallas_call(
        kern,
        grid=(n_tok // tm,),
        in_specs=[
            pl.BlockSpec((tm, d_model), lambda i: (i, 0)),
            pl.BlockSpec((1, d_model), lambda i: (0, 0)),
            pl.BlockSpec((1, 1, d_model), lambda i: (i // tpb, 0, 0)),
            pl.BlockSpec((1, 1, d_model), lambda i: (i // tpb, 0, 0)),
            pl.BlockSpec((d_model, LANES), lambda i: (0, 0)),
            pl.BlockSpec((1, LANES), lambda i: (0, 0)),
        ],
        out_specs=[
            pl.BlockSpec((tm * (half // LANES), LANES), lambda i: (i, 0)),
            pl.BlockSpec((tm, LANES), lambda i: (i, 0)),
            pl.BlockSpec((tm, LANES), lambda i: (i, 0)),
            pl.BlockSpec((tm, LANES), lambda i: (i, 0)),
            pl.BlockSpec((1, LANES), lambda i: (0, 0)),
        ],
        out_shape=[
            jax.ShapeDtypeStruct((n_tok * (half // LANES), LANES), jnp.uint32),
            jax.ShapeDtypeStruct((n_tok, LANES), jnp.int32),
            jax.ShapeDtypeStruct((n_tok, LANES), F32),
            jax.ShapeDtypeStruct((n_tok, LANES), jnp.int32),
            jax.ShapeDtypeStruct((1, LANES), F32),
        ],
        scratch_shapes=[pltpu.VMEM((tm, d_model), BF16), pltpu.VMEM((1, LANES), F32)],
        compiler_params=_cparams("arbitrary"),
        name="norm_router_topk",
    )(x2d, gain, scale, shift, rw, rb)


def _row_token_kernel(dest_ref, zeros_ref, o_ref, sem, *, blk, xp):
    i = pl.program_id(0)

    @pl.when(i == 0)
    def _init():
        fill = pltpu.make_async_copy(zeros_ref, o_ref, sem)
        fill.start()
        fill.wait()

    base = i * blk

    def place(g, carry):
        for u in range(SCALAR_UNROLL):
            a = g * SCALAR_UNROLL + u
            o_ref[dest_ref[a]] = lax.shift_right_logical(base + a, TOP_K.bit_length() - 1) * xp
        return carry
    lax.fori_loop(0, blk // SCALAR_UNROLL, place, 0)


def _row_tokens(dest, n_rows, xp, t):
    n_assign = dest.shape[0]
    blk = min(t["inv_blk"], n_assign)
    assert TOP_K & (TOP_K - 1) == 0 and n_assign % blk == 0 and blk % SCALAR_UNROLL == 0
    return pl.pallas_call(
        functools.partial(_row_token_kernel, blk=blk, xp=xp),
        grid=(n_assign // blk,),
        in_specs=[pl.BlockSpec((blk,), lambda i: (i,), memory_space=pltpu.SMEM),
                  pl.BlockSpec(memory_space=pl.ANY)],
        out_specs=pl.BlockSpec(memory_space=pltpu.SMEM),
        out_shape=jax.ShapeDtypeStruct((n_rows,), jnp.int32),
        scratch_shapes=[pltpu.SemaphoreType.DMA],
        compiler_params=_cparams("arbitrary"),
        name="moe_row_tokens",
    )(dest, jnp.zeros((n_rows,), jnp.int32))


def _expert_kernel(be_ref, na_ref, tok_ref, nxt_ref, src_ref, w1_ref, b1_ref, w2_ref, b2_ref,
                   o_ref, xbuf, sems, *, bm, half, fdim, xp, dp):
    del be_ref
    i = pl.program_id(0)
    n_active = na_ref[0]

    def slab_copy(offs, r, slot):
        src0 = pl.multiple_of(offs[r], xp)
        dst0 = pl.multiple_of(r * dp, SUBLANES)
        return pltpu.make_async_copy(src_ref.at[pl.ds(src0, xp), :],
                                     xbuf.at[slot, pl.ds(dst0, xp), :], sems.at[slot])

    def gather_start(toks, slot):
        def issue(g, carry):
            for u in range(DMA_UNROLL):
                slab_copy(toks, g * DMA_UNROLL + u, slot).start(priority=u % 2)
            return carry
        lax.fori_loop(0, bm // DMA_UNROLL, issue, 0)

    def gather_wait(toks, slot):
        def drain(g, carry):
            for u in range(DMA_UNROLL):
                slab_copy(toks, g * DMA_UNROLL + u, slot).wait()
            return carry
        lax.fori_loop(0, bm // DMA_UNROLL, drain, 0)

    def compute(slot):
        words = jnp.concatenate([xbuf[slot, pl.ds(s, bm, stride=dp), :] for s in range(xp)], axis=1)
        lo = lax.bitcast_convert_type(lax.shift_left(words, jnp.uint32(16)), F32).astype(BF16)
        hi = lax.bitcast_convert_type(words & jnp.uint32(0xFFFF0000), F32).astype(BF16)
        hid = (jnp.dot(lo, w1_ref[0, :half, :], preferred_element_type=F32)
               + jnp.dot(hi, w1_ref[0, half:, :], preferred_element_type=F32)) + b1_ref[0]
        glu = jnp.minimum(hid[:, :fdim], SWIGLU_LIMIT)
        lin = jnp.clip(hid[:, fdim:], -SWIGLU_LIMIT, SWIGLU_LIMIT)
        act = (glu / (1.0 + jnp.exp(-SWIGLU_ALPHA * glu))) * (lin + 1.0)
        y = jnp.dot(act.astype(BF16), w2_ref[0], preferred_element_type=F32) + b2_ref[0]
        bits = lax.bitcast_convert_type(y.astype(BF16).astype(F32), jnp.uint32)
        out_words = lax.shift_right_logical(bits[:, :half], jnp.uint32(16)) | (
            bits[:, half:] & jnp.uint32(0xFFFF0000))
        for c in range(xp):
            o_ref[pl.ds(c, bm, stride=xp), :] = out_words[:, c * LANES:(c + 1) * LANES]

    @pl.when(i == 0)
    def _first():
        gather_start(tok_ref, 0)

    for slot in range(2):
        @pl.when((i % 2 == slot) & (i + 1 < n_active))
        def _prefetch(slot=slot):
            gather_start(nxt_ref, 1 - slot)

        @pl.when((i % 2 == slot) & (i < n_active))
        def _active(slot=slot):
            gather_wait(tok_ref, slot)
            compute(slot)

    @pl.when(i >= n_active)
    def _unused():
        o_ref[...] = jnp.zeros(o_ref.shape, o_ref.dtype)


def _vmem_pitch(rows):
    groups = -(-rows // SUBLANES)
    return (groups + 1 - groups % 2) * SUBLANES


def _experts(block_e, n_active, row_tok, packed, w1, b1, w2, b2, *, bm, xp):
    dp = _vmem_pitch(xp)
    n_blocks = row_tok.shape[0] // bm
    n_exp, d_model, two_f = w1.shape
    fdim, half = two_f // 2, d_model // 2
    grid_spec = pltpu.PrefetchScalarGridSpec(
        num_scalar_prefetch=2,
        grid=(n_blocks,),
        in_specs=[
            pl.BlockSpec((bm,), lambda i, be, na: (i,), memory_space=pltpu.SMEM),
            pl.BlockSpec((bm,), lambda i, be, na: (jnp.minimum(i + 1, n_blocks - 1),),
                         memory_space=pltpu.SMEM),
            pl.BlockSpec(memory_space=pl.ANY),
            pl.BlockSpec((1, d_model, two_f), lambda i, be, na: (be[i], 0, 0)),
            pl.BlockSpec((1, 1, two_f), lambda i, be, na: (be[i], 0, 0)),
            pl.BlockSpec((1, fdim, d_model), lambda i, be, na: (be[i], 0, 0)),
            pl.BlockSpec((1, 1, d_model), lambda i, be, na: (be[i], 0, 0)),
        ],
        out_specs=pl.BlockSpec((bm * xp, LANES), lambda i, be, na: (i, 0)),
        scratch_shapes=[pltpu.VMEM((2, bm * dp, LANES), packed.dtype),
                        pltpu.SemaphoreType.DMA((2,))],
    )
    return pl.pallas_call(
        functools.partial(_expert_kernel, bm=bm, half=half, fdim=fdim, xp=xp, dp=dp),
        grid_spec=grid_spec,
        out_shape=jax.ShapeDtypeStruct((n_blocks * bm * xp, LANES), packed.dtype),
        compiler_params=_cparams("arbitrary"),
        name="moe_experts",
    )(block_e, n_active, row_tok, row_tok, packed, w1, b1.reshape(n_exp, 1, two_f),
      w2, b2.reshape(n_exp, 1, d_model))


def _combine_kernel(dest_ref, nxt_ref, y_ref, wts_ref, x_ref, g_ref, *rest, tm, yp, dp, half, n_steps,
                    next_norm):
    if next_norm:
        ng_ref, nsc_ref, nsh_ref, o_ref, hn_ref, buf, sems = rest
    else:
        o_ref, buf, sems = rest
    i = pl.program_id(0)

    def slab_copy(offs, r, k, slot):
        src0 = pl.multiple_of(offs[r * TOP_K + k], yp)
        dst0 = pl.multiple_of(r * dp, SUBLANES)
        return pltpu.make_async_copy(y_ref.at[pl.ds(src0, yp), :],
                                     buf.at[slot, k, pl.ds(dst0, yp), :], sems.at[slot])

    def gather_start(dests, slot):
        def issue(g, carry):
            for u in range(DMA_UNROLL // TOP_K):
                for k in range(TOP_K):
                    slab_copy(dests, g * (DMA_UNROLL // TOP_K) + u, k, slot).start(priority=k % 2)
            return carry
        lax.fori_loop(0, tm * TOP_K // DMA_UNROLL, issue, 0)

    def gather_wait(dests, slot):
        def drain(g, carry):
            for u in range(DMA_UNROLL // TOP_K):
                for k in range(TOP_K):
                    slab_copy(dests, g * (DMA_UNROLL // TOP_K) + u, k, slot).wait()
            return carry
        lax.fori_loop(0, tm * TOP_K // DMA_UNROLL, drain, 0)

    def compute(slot):
        w = wts_ref[...]
        wk = [jnp.broadcast_to(w[:, k:k + 1], (tm, LANES)) for k in range(TOP_K)]
        for c in range(yp):
            acc_lo = acc_hi = None
            for k in range(TOP_K):
                words = buf[slot, k, pl.ds(c, tm, stride=dp), :]
                lo = wk[k] * lax.bitcast_convert_type(lax.shift_left(words, jnp.uint32(16)), F32)
                hi = wk[k] * lax.bitcast_convert_type(words & jnp.uint32(0xFFFF0000), F32)
                acc_lo = lo if acc_lo is None else acc_lo + lo
                acc_hi = hi if acc_hi is None else acc_hi + hi
            for base, acc in ((0, acc_lo), (half, acc_hi)):
                sl = slice(base + c * LANES, base + (c + 1) * LANES)
                o_ref[:, sl] = x_ref[:, sl] + g_ref[0, :, sl] * acc
        if next_norm:
            x_new = o_ref[...]
            ms = jnp.mean(x_new * x_new, axis=-1, keepdims=True)
            y = x_new * lax.rsqrt(ms + NORM_EPS)
            hn_ref[...] = ((y * ng_ref[...]) * (1.0 + nsc_ref[0]) + nsh_ref[0]).astype(BF16)

    @pl.when(i == 0)
    def _first():
        gather_start(dest_ref, 0)

    for slot in range(2):
        @pl.when((i % 2 == slot) & (i + 1 < n_steps))
        def _prefetch(slot=slot):
            gather_start(nxt_ref, 1 - slot)

        @pl.when(i % 2 == slot)
        def _active(slot=slot):
            gather_wait(dest_ref, slot)
            compute(slot)


def _combine(dest_off, ys, yp, wts, x2d, gate, next_norm, *, seq, t):
    n_tok, d_model = x2d.shape
    tm = t["cmb_tm"]
    dp = _vmem_pitch(yp)
    tpb = seq // tm
    n_steps = n_tok // tm
    row_spec = pl.BlockSpec((tm, d_model), lambda i: (i, 0))
    mod_spec = pl.BlockSpec((1, 1, d_model), lambda i: (i // tpb, 0, 0))
    in_specs = [
        pl.BlockSpec((tm * TOP_K,), lambda i: (i,), memory_space=pltpu.SMEM),
        pl.BlockSpec((tm * TOP_K,), lambda i: (jnp.minimum(i + 1, n_steps - 1),),
                     memory_space=pltpu.SMEM),
        pl.BlockSpec(memory_space=pl.ANY),
        pl.BlockSpec((tm, LANES), lambda i: (i, 0)),
        row_spec,
        mod_spec,
    ]
    out_specs, out_shape = [row_spec], [jax.ShapeDtypeStruct((n_tok, d_model), F32)]
    args = [dest_off, dest_off, ys, wts, x2d, gate]
    if next_norm is not None:
        in_specs += [pl.BlockSpec((1, d_model), lambda i: (0, 0)), mod_spec, mod_spec]
        out_specs.append(row_spec)
        out_shape.append(jax.ShapeDtypeStruct((n_tok, d_model), BF16))
        args += list(next_norm)
    outs = pl.pallas_call(
        functools.partial(_combine_kernel, tm=tm, yp=yp, dp=dp, half=d_model // 2, n_steps=n_steps,
                          next_norm=next_norm is not None),
        grid=(n_steps,),
        in_specs=in_specs,
        out_specs=out_specs,
        out_shape=out_shape,
        scratch_shapes=[pltpu.VMEM((2, TOP_K, tm * dp, LANES), ys.dtype), pltpu.SemaphoreType.DMA((2,))],
        compiler_params=_cparams("arbitrary"),
        name="moe_combine_residual",
    )(*args)
    return (outs[0], outs[1]) if next_norm is not None else (outs[0], None)


def _layer(layer, x2d, h1, mod, next_norm, w_in_all, forget_b, pool_w, pool_scale, q_norm_g,
           k_norm_g, w_out_all, norm2_g, router_w, router_b, w1_f32, b1, w2_f32, b2, *, seq, t):
    n_tok, d_model = x2d.shape
    batch = n_tok // seq
    n_groups, gd, _ = pool_w.shape
    pool_width = n_groups * gd
    att_width = d_model - pool_width
    n_heads = att_width // HEAD_DIM
    n_exp = router_w.shape[1]
    assert pool_width == att_width and n_groups == len(POOL_WINDOWS) and gd % HEAD_DIM == 0
    n_main = pool_width + 3 * att_width
    hp = -(-n_heads // BF16_ROWS) * BF16_ROWS

    shift1, scale1, gate1, shift2, scale2, gate2 = [
        m.reshape(batch, 1, d_model) for m in jnp.split(mod, N_MOD, axis=-1)]

    wf_t = jnp.zeros((hp, d_model), BF16).at[:n_heads].set(w_in_all[layer, n_main:, :])
    fb = jnp.zeros((hp, 1), F32).at[:n_heads, 0].set(forget_b)
    reps = gd // HEAD_DIM
    main, cum = _in_projection(
        h1, w_in_all, layer, n_main, wf_t, fb,
        pool_w.astype(BF16), pool_scale.reshape(n_groups, 1, gd),
        jnp.tile(q_norm_g, reps).reshape(1, gd), jnp.tile(k_norm_g, reps).reshape(1, gd),
        seq=seq, t=t)
    att, w1, w2 = _attention(main, cum, w1_f32, w2_f32, layer, seq=seq, n_heads=n_heads,
                             q_off=pool_width, k_off=pool_width + att_width,
                             v_off=pool_width + 2 * att_width, t=t)
    x1 = _out_projection(main, att, w_out_all, layer, x2d, gate1, seq=seq,
                         pool_width=pool_width, t=t)

    rw = jnp.zeros((d_model, LANES), BF16).at[:, :n_exp].set(router_w.astype(BF16))
    rb = jnp.full((1, LANES), -jnp.inf, F32).at[0, :n_exp].set(router_b)
    packed, idx, wts, rank, counts = _router(
        x1, norm2_g.reshape(1, d_model), scale2, shift2, rw, rb, seq=seq, t=t)

    bm = t["moe_bm"]
    n_assign = n_tok * TOP_K
    n_blocks = -(-n_assign // bm) + n_exp
    counts = counts[0, :n_exp].astype(jnp.int32)
    padded = (counts + bm - 1) // bm * bm
    pad_end = jnp.cumsum(padded)
    pad_start = pad_end - padded
    dest = (pad_start[idx[:, :TOP_K]] + rank[:, :TOP_K]).reshape(-1)
    xp = d_model // 2 // LANES
    row_tok = _row_tokens(dest, n_blocks * bm, xp, t)
    block_start = jnp.arange(n_blocks, dtype=jnp.int32) * bm
    block_e = jnp.minimum(jnp.sum(block_start[:, None] >= pad_end[None, :], axis=1),
                          n_exp - 1).astype(jnp.int32)
    n_active = (pad_end[-1:] // bm).astype(jnp.int32)

    ys = _experts(block_e, n_active, row_tok, packed, w1, b1, w2, b2, bm=bm, xp=xp)
    return _combine(dest * xp, ys, xp, wts, x1, gate2, next_norm, seq=seq, t=t)


def kernel(x, c, ada_w, ada_b, norm1_g, w_in, forget_b, pool_w, pool_scale, q_norm_g, k_norm_g,
           w_out, norm2_g, router_w, router_b, expert_w1, expert_b1, expert_w2, expert_b2):
    batch, seq, d_model = x.shape
    depth = ada_w.shape[0]
    t = _tiles(batch * seq, seq, d_model)
    mod = _modulation(c, ada_w, ada_b, t["mod_tn"])
    x2d = x.reshape(batch * seq, d_model)
    proj_width = w_in.shape[2]
    in_rows = max([r for r in range(BF16_ROWS, t["cast_rows"] * 2 + 1, BF16_ROWS) if proj_width % r == 0]
                  or [proj_width])
    w_in_all = _cast_bf16(jnp.swapaxes(w_in, 1, 2), in_rows)
    w_out_all = _cast_bf16(w_out, t["cast_rows"])

    def first_norm(l):
        shift1, scale1 = jnp.split(mod[l], N_MOD, axis=-1)[:2]
        return (norm1_g[l].reshape(1, d_model), scale1.reshape(batch, 1, d_model),
                shift1.reshape(batch, 1, d_model))

    h1 = _norm_modulate(x2d, *first_norm(0), seq=seq, t=t)
    for l in range(depth):
        next_norm = first_norm(l + 1) if l + 1 < depth else None
        x2d, h1 = _layer(l, x2d, h1, mod[l], next_norm, w_in_all, forget_b[l], pool_w[l],
                         pool_scale[l], q_norm_g[l], k_norm_g[l], w_out_all, norm2_g[l], router_w[l],
                         router_b[l], expert_w1, expert_b1[l], expert_w2, expert_b2[l], seq=seq, t=t)
    return x2d.reshape(batch, seq, d_model)
```

```python
import functools

import jax
import jax.numpy as jnp
from jax import lax
from jax.experimental import pallas as pl
from jax.experimental.pallas import tpu as pltpu

HEAD_DIM = 128
POOL_WINDOWS = (2, 4, 8, 16)
TOP_K = 4
N_MOD = 6
NORM_EPS = 1e-6
SWIGLU_ALPHA = 1.702
SWIGLU_LIMIT = 7.0
LOG2_E = 1.4426950408889634
ATT_LOGIT_SCALE = HEAD_DIM ** -0.5 * LOG2_E

LANES = 128
SUBLANES = 8
BF16_ROWS = 16
SCALAR_UNROLL = 8
DMA_UNROLL = 32
POOL_HALO = 16
VMEM_LIMIT_BYTES = 56 * 2**20

F32 = jnp.float32
BF16 = jnp.bfloat16


def _cparams(*sem):
    return pltpu.CompilerParams(dimension_semantics=sem, vmem_limit_bytes=VMEM_LIMIT_BYTES)


def _tiles(n_tok, seq, d_model):
    return dict(
        mod_tn=min(512, d_model),
        cast_rows=min(256, d_model),
        norm_tm=min(512, seq),
        in_tm=min(1024, seq),
        norm_rc=min(64, seq),
        att_tq=min(1024, seq),
        att_tk=1024,
        att_heads=2,
        out_tm=min(1024, seq),
        out_tn=min(1024, d_model),
        rt_tm=min(512, seq),
        moe_bm=min(256, n_tok),
        inv_blk=8192,
        cmb_tm=min(128, seq),
    )


def _cast_kernel(x_ref, o_ref):
    o_ref[...] = x_ref[...].astype(BF16)


def _cast_bf16(w, rows):
    depth, n_rows, n_cols = w.shape
    spec = pl.BlockSpec((1, rows, n_cols), lambda l, i: (l, i, 0))
    return pl.pallas_call(
        _cast_kernel,
        grid=(depth, n_rows // rows),
        in_specs=[spec],
        out_specs=spec,
        out_shape=jax.ShapeDtypeStruct(w.shape, BF16),
        compiler_params=_cparams("arbitrary", "arbitrary"),
        name="weight_cast",
    )(w)


def _mod_kernel(c_ref, w_ref, b_ref, o_ref):
    c = c_ref[...]
    ca = c / (1.0 + jnp.exp(-c))
    w = w_ref[0].astype(BF16)
    o_ref[0] = jnp.dot(ca.astype(BF16), w, preferred_element_type=F32) + b_ref[0]


def _modulation(c, ada_w, ada_b, tn):
    depth, d_model, n_out = ada_w.shape
    b = c.shape[0]
    c_pad = jnp.zeros((SUBLANES, d_model), F32).at[:b].set(c)
    out = pl.pallas_call(
        _mod_kernel,
        grid=(depth, n_out // tn),
        in_specs=[
            pl.BlockSpec((SUBLANES, d_model), lambda l, j: (0, 0)),
            pl.BlockSpec((1, d_model, tn), lambda l, j: (l, 0, j)),
            pl.BlockSpec((1, 1, tn), lambda l, j: (l, 0, j)),
        ],
        out_specs=pl.BlockSpec((1, SUBLANES, tn), lambda l, j: (l, 0, j)),
        out_shape=jax.ShapeDtypeStruct((depth, SUBLANES, n_out), F32),
        compiler_params=_cparams("arbitrary", "arbitrary"),
        name="adaln_mod",
    )(c_pad, ada_w, ada_b.reshape(depth, 1, n_out))
    return out[:, :b]


def _norm_kernel(x_ref, g_ref, sc_ref, sh_ref, h_ref, *, tm, rc):
    def body(c, carry):
        r0 = pl.multiple_of(c * rc, rc)
        x = x_ref[pl.ds(r0, rc), :]
        ms = jnp.mean(x * x, axis=-1, keepdims=True)
        y = x * lax.rsqrt(ms + NORM_EPS)
        h = (y * g_ref[...]) * (1.0 + sc_ref[0]) + sh_ref[0]
        h_ref[pl.ds(r0, rc), :] = h.astype(BF16)
        return carry
    lax.fori_loop(0, tm // rc, body, 0)


def _norm_modulate(x2d, gain, scale, shift, *, seq, t):
    n_tok, d_model = x2d.shape
    tm, rc = t["norm_tm"], t["norm_rc"]
    tpb = seq // tm
    return pl.pallas_call(
        functools.partial(_norm_kernel, tm=tm, rc=rc),
        grid=(n_tok // tm,),
        in_specs=[
            pl.BlockSpec((tm, d_model), lambda i: (i, 0)),
            pl.BlockSpec((1, d_model), lambda i: (0, 0)),
            pl.BlockSpec((1, 1, d_model), lambda i: (i // tpb, 0, 0)),
            pl.BlockSpec((1, 1, d_model), lambda i: (i // tpb, 0, 0)),
        ],
        out_specs=pl.BlockSpec((tm, d_model), lambda i: (i, 0)),
        out_shape=jax.ShapeDtypeStruct((n_tok, d_model), BF16),
        compiler_params=_cparams("arbitrary"),
        name="norm_modulate",
    )(x2d, gain, scale, shift)


def _in_kernel(h_ref, w_ref, wf_ref, fb_ref, pw_ref, ps_ref, qg_ref, kg_ref,
               main_ref, cum_ref, tail_sc, fc_sc, *, tm, tpb, gd):
    i = pl.program_id(0)
    j = pl.program_id(1)
    first = (i % tpb) == 0

    def proj():
        return lax.dot_general(h_ref[...], w_ref[...], (((1,), (1,)), ((), ())),
                               preferred_element_type=F32)

    for g, win in enumerate(POOL_WINDOWS):
        @pl.when(j == g)
        def _pool(g=g, win=win):
            u = proj()
            prev = jnp.where(first, 0.0, tail_sc[g])
            s = jnp.concatenate([prev, u], axis=0)
            shift = 1
            while shift < win:
                s = s + pltpu.roll(s, shift, 0)
                shift *= 2
            wsum = s[POOL_HALO:]
            pos = (i % tpb) * tm + lax.broadcasted_iota(jnp.int32, (tm, 1), 0)
            cnt = jnp.minimum(pos + 1, win).astype(F32)
            mixed = wsum / cnt - u
            po = jnp.dot(mixed.astype(BF16), pw_ref[0], preferred_element_type=F32) * ps_ref[0]
            main_ref[...] = po.astype(BF16)
            tail_sc[g] = u[tm - POOL_HALO:]

    def qk_norm(gain_ref, post_scale):
        r = proj()
        for c in range(gd // HEAD_DIM):
            sl = slice(c * HEAD_DIM, (c + 1) * HEAD_DIM)
            rc_ = r[:, sl]
            ms = jnp.mean(rc_ * rc_, axis=-1, keepdims=True)
            normed = (rc_ * lax.rsqrt(ms + NORM_EPS)) * gain_ref[:, sl]
            if post_scale is not None:
                normed = normed * post_scale
            main_ref[:, sl] = normed.astype(BF16)

    @pl.when((j >= 4) & (j < 8))
    def _q():
        qk_norm(qg_ref, ATT_LOGIT_SCALE)

    @pl.when((j >= 8) & (j < 12))
    def _k():
        qk_norm(kg_ref, None)

    @pl.when((j >= 12) & (j < 16))
    def _v():
        main_ref[...] = proj().astype(BF16)

    @pl.when(j == 16)
    def _forget():
        z = lax.dot_general(wf_ref[...], h_ref[...], (((1,), (1,)), ((), ())),
                            preferred_element_type=F32) + fb_ref[...]
        ls = jnp.minimum(z, 0.0) - jnp.log1p(jnp.exp(-jnp.abs(z)))
        p0 = ls.astype(BF16)
        r1 = ls - p0.astype(F32)
        p1 = r1.astype(BF16)
        p2 = (r1 - p1.astype(F32)).astype(BF16)
        rr = lax.broadcasted_iota(jnp.int32, (tm, tm), 0)
        cc = lax.broadcasted_iota(jnp.int32, (tm, tm), 1)
        tri = jnp.where(rr <= cc, 1.0, 0.0).astype(BF16)
        cum = (jnp.dot(p0, tri, preferred_element_type=F32)
               + jnp.dot(p1, tri, preferred_element_type=F32)
               + jnp.dot(p2, tri, preferred_element_type=F32))
        cum = cum + jnp.where(first, 0.0, fc_sc[:, 0:1])
        cum_ref[0] = cum * LOG2_E
        fc_sc[...] = jnp.broadcast_to(cum[:, tm - 1:tm], fc_sc.shape)


def _in_projection(h2d, w_in_all, layer, n_main, wf_t, fb, pool_w, pool_scale, qg, kg, *, seq, t):
    n_tok, d_model = h2d.shape
    batch = n_tok // seq
    tm = t["in_tm"]
    gd = pool_w.shape[-1]
    hp = wf_t.shape[0]
    tpb = seq // tm
    nj = n_main // gd + 1
    last = n_main // gd - 1
    kern = functools.partial(_in_kernel, tm=tm, tpb=tpb, gd=gd)
    return pl.pallas_call(
        kern,
        grid=(n_tok // tm, nj),
        in_specs=[
            pl.BlockSpec((tm, d_model), lambda i, j: (i, 0)),
            pl.BlockSpec((None, gd, d_model), lambda i, j: (layer, jnp.minimum(j, last), 0)),
            pl.BlockSpec((hp, d_model), lambda i, j: (0, 0)),
            pl.BlockSpec((hp, 1), lambda i, j: (0, 0)),
            pl.BlockSpec((1, gd, gd), lambda i, j: (jnp.minimum(j, 3), 0, 0)),
            pl.BlockSpec((1, 1, gd), lambda i, j: (jnp.minimum(j, 3), 0, 0)),
            pl.BlockSpec((1, gd), lambda i, j: (0, 0)),
            pl.BlockSpec((1, gd), lambda i, j: (0, 0)),
        ],
        out_specs=[
            pl.BlockSpec((tm, gd), lambda i, j: (i, jnp.minimum(j, last))),
            pl.BlockSpec((1, hp, tm), lambda i, j: (i // tpb, 0, i % tpb)),
        ],
        out_shape=[
            jax.ShapeDtypeStruct((n_tok, n_main), BF16),
            jax.ShapeDtypeStruct((batch, hp, seq), F32),
        ],
        scratch_shapes=[
            pltpu.VMEM((len(POOL_WINDOWS), POOL_HALO, gd), F32),
            pltpu.VMEM((hp, LANES), F32),
        ],
        compiler_params=_cparams("arbitrary", "arbitrary"),
        name="in_proj",
    )(h2d, w_in_all, wf_t, fb, pool_w, pool_scale, qg, kg)


def _attn_kernel(q_ref, k_ref, v_ref, c_ref, w1_ref, w2_ref, o_ref, w1o_ref, w2o_ref, *, tq, tk, hps):
    w1o_ref[...] = w1_ref[...].astype(BF16)
    w2o_ref[...] = w2_ref[...].astype(BF16)
    qi = pl.program_id(2)
    heads = [slice(h * HEAD_DIM, (h + 1) * HEAD_DIM) for h in range(hps)]
    kpq = tq // tk

    def block(h, kj, carry, diag_index):
        m_prev, l_prev, acc_prev = carry
        start = pl.multiple_of(kj * tk, tk)
        k = k_ref[pl.ds(start, tk), heads[h]]
        v = v_ref[pl.ds(start, tk), heads[h]]
        s = lax.dot_general(q_ref[:, heads[h]], k, (((1,), (1,)), ((), ())),
                            preferred_element_type=F32)
        s = s - c_ref[h, :, pl.ds(start, tk)]
        if diag_index is not None:
            rr = lax.broadcasted_iota(jnp.int32, (tq, tk), 0)
            cc = lax.broadcasted_iota(jnp.int32, (tq, tk), 1)
            s = jnp.where(cc + diag_index * tk <= rr, s, -jnp.inf)
        m_new = jnp.maximum(m_prev, jnp.max(s, axis=-1, keepdims=True))
        p = jnp.exp2(s - m_new)
        alpha = jnp.exp2(m_prev - m_new)
        l_new = alpha * l_prev + jnp.sum(p, axis=-1, keepdims=True)
        acc_new = alpha * acc_prev + jnp.dot(p.astype(BF16), v, preferred_element_type=F32)
        return m_new, l_new, acc_new

    def full_blocks(g, carry):
        for d in range(kpq):
            carry = tuple(block(h, g * kpq + d, carry[h], None) for h in range(hps))
        return carry

    def diagonal_blocks(_, carry):
        for d in range(kpq):
            carry = tuple(block(h, qi * kpq + d, carry[h], d) for h in range(hps))
        return carry

    init = tuple((jnp.full((tq, 1), -jnp.inf, F32), jnp.zeros((tq, 1), F32),
                  jnp.zeros((tq, HEAD_DIM), F32)) for _ in range(hps))
    carry = lax.fori_loop(0, qi, full_blocks, init)
    carry = lax.fori_loop(0, jnp.minimum(qi + 1, 1), diagonal_blocks, carry)
    for h in range(hps):
        _, l_f, acc_f = carry[h]
        o_ref[:, heads[h]] = (acc_f / l_f).astype(o_ref.dtype)


def _attention(main, cum, w1_f32, w2_f32, layer, *, seq, n_heads, q_off, k_off, v_off, t):
    n_tok = main.shape[0]
    batch = n_tok // seq
    tq, hps = t["att_tq"], min(t["att_heads"], n_heads)
    nq = seq // tq
    hp = cum.shape[1]
    width = hps * HEAD_DIM
    assert n_heads % hps == 0 and hp % hps == 0
    cum3 = cum.reshape(batch * hp, 1, seq)
    kern = functools.partial(_attn_kernel, tq=tq, tk=min(t["att_tk"], tq), hps=hps)
    qb, kb, vb, cb = q_off // width, k_off // width, v_off // width, hp // hps
    n_hg = n_heads // hps
    n_steps = batch * n_hg * nq
    depth, n_exp, d_model, two_f = w1_f32.shape
    fdim = w2_f32.shape[2]
    rows1, rows2 = n_exp * d_model, n_exp * fdim
    assert rows1 % (n_steps * BF16_ROWS) == 0 and rows2 % (n_steps * BF16_ROWS) == 0
    r1, r2 = rows1 // n_steps, rows2 // n_steps

    def step(b, h, i):
        return (b * n_hg + h) * nq + i

    att, w1b, w2b = pl.pallas_call(
        kern,
        grid=(batch, n_hg, nq),
        in_specs=[
            pl.BlockSpec((tq, width), lambda b, h, i: (b * nq + i, qb + h)),
            pl.BlockSpec((seq, width), lambda b, h, i: (b, kb + h)),
            pl.BlockSpec((seq, width), lambda b, h, i: (b, vb + h)),
            pl.BlockSpec((hps, 1, seq), lambda b, h, i: (b * cb + h, 0, 0)),
            pl.BlockSpec((None, r1, two_f), lambda b, h, i: (layer, step(b, h, i), 0)),
            pl.BlockSpec((None, r2, d_model), lambda b, h, i: (layer, step(b, h, i), 0)),
        ],
        out_specs=[
            pl.BlockSpec((tq, width), lambda b, h, i: (b * nq + i, h)),
            pl.BlockSpec((r1, two_f), lambda b, h, i: (step(b, h, i), 0)),
            pl.BlockSpec((r2, d_model), lambda b, h, i: (step(b, h, i), 0)),
        ],
        out_shape=[
            jax.ShapeDtypeStruct((n_tok, n_heads * HEAD_DIM), BF16),
            jax.ShapeDtypeStruct((rows1, two_f), BF16),
            jax.ShapeDtypeStruct((rows2, d_model), BF16),
        ],
        compiler_params=_cparams("arbitrary", "arbitrary", "arbitrary"),
        name="forget_attention",
    )(main, main, main, cum3, w1_f32.reshape(depth, rows1, two_f), w2_f32.reshape(depth, rows2, d_model))
    return att, w1b.reshape(n_exp, d_model, two_f), w2b.reshape(n_exp, fdim, d_model)


def _out_kernel(po_ref, at_ref, wp_ref, wa_ref, x_ref, g_ref, o_ref):
    mix = (jnp.dot(po_ref[...], wp_ref[...], preferred_element_type=F32)
           + jnp.dot(at_ref[...], wa_ref[...], preferred_element_type=F32))
    o_ref[...] = x_ref[...] + g_ref[0] * mix


def _out_projection(main, att, w_out_all, layer, x2d, gate, *, seq, pool_width, t):
    n_tok, d_model = x2d.shape
    tm, tn = t["out_tm"], t["out_tn"]
    att_width = att.shape[1]
    tpb = seq // tm
    return pl.pallas_call(
        _out_kernel,
        grid=(n_tok // tm, d_model // tn),
        in_specs=[
            pl.BlockSpec((tm, pool_width), lambda i, j: (i, 0)),
            pl.BlockSpec((tm, att_width), lambda i, j: (i, 0)),
            pl.BlockSpec((None, pool_width, tn), lambda i, j: (layer, 0, j)),
            pl.BlockSpec((None, att_width, tn), lambda i, j: (layer, pool_width // att_width, j)),
            pl.BlockSpec((tm, tn), lambda i, j: (i, j)),
            pl.BlockSpec((1, 1, tn), lambda i, j: (i // tpb, 0, j)),
        ],
        out_specs=pl.BlockSpec((tm, tn), lambda i, j: (i, j)),
        out_shape=jax.ShapeDtypeStruct((n_tok, d_model), F32),
        compiler_params=_cparams("arbitrary", "arbitrary"),
        name="out_proj_residual",
    )(main, att, w_out_all, w_out_all, x2d, gate)


def _router_kernel(x_ref, g_ref, sc_ref, sh_ref, rw_ref, rb_ref,
                   hp_ref, idx_ref, wts_ref, rank_ref, cnt_ref, h_sc, cnt_sc, *, tm, rc, half):
    i = pl.program_id(0)

    @pl.when(i == 0)
    def _init():
        cnt_sc[...] = jnp.zeros(cnt_sc.shape, F32)

    def body(c, carry):
        r0 = pl.multiple_of(c * rc, rc)
        x = x_ref[pl.ds(r0, rc), :]
        ms = jnp.mean(x * x, axis=-1, keepdims=True)
        y = x * lax.rsqrt(ms + NORM_EPS)
        h = ((y * g_ref[...]) * (1.0 + sc_ref[0]) + sh_ref[0]).astype(BF16)
        h_sc[pl.ds(r0, rc), :] = h
        bits = lax.bitcast_convert_type(h.astype(F32), jnp.uint32)
        lo = lax.shift_right_logical(bits[:, :half], jnp.uint32(16))
        hi = bits[:, half:] & jnp.uint32(0xFFFF0000)
        words = lo | hi
        pitch = half // LANES
        for s in range(pitch):
            hp_ref[pl.ds(r0 * pitch + s, rc, stride=pitch), :] = words[:, s * LANES:(s + 1) * LANES]
        return carry
    lax.fori_loop(0, tm // rc, body, 0)

    logits = jnp.dot(h_sc[...], rw_ref[...], preferred_element_type=F32) + rb_ref[...]
    lane = lax.broadcasted_iota(jnp.int32, logits.shape, 1).astype(F32)
    vals, sels, hots = [], [], []
    cur = logits
    for _ in range(TOP_K):
        mx = jnp.max(cur, axis=-1, keepdims=True)
        sel = jnp.min(jnp.where(cur == mx, lane, float(LANES)), axis=-1, keepdims=True)
        hot = lane == sel
        vals.append(mx)
        sels.append(sel)
        hots.append(hot)
        cur = jnp.where(hot, -jnp.inf, cur)
    exps = [jnp.exp(v - vals[0]) for v in vals]
    denom = exps[0] + exps[1] + exps[2] + exps[3]

    hot_sum = jnp.zeros(logits.shape, F32)
    for hot in hots:
        hot_sum = hot_sum + jnp.where(hot, 1.0, 0.0)
    rr = lax.broadcasted_iota(jnp.int32, (tm, tm), 0)
    cc = lax.broadcasted_iota(jnp.int32, (tm, tm), 1)
    below = jnp.where(cc < rr, 1.0, 0.0).astype(BF16)
    base = jnp.dot(below, hot_sum.astype(BF16), preferred_element_type=F32) + cnt_sc[...]

    idx_out = jnp.zeros(logits.shape, F32)
    wts_out = jnp.zeros(logits.shape, F32)
    rank_out = jnp.zeros(logits.shape, F32)
    for k in range(TOP_K):
        col = lane == float(k)
        rank_k = jnp.sum(jnp.where(hots[k], base, 0.0), axis=-1, keepdims=True)
        idx_out = jnp.where(col, sels[k], idx_out)
        wts_out = jnp.where(col, exps[k] / denom, wts_out)
        rank_out = jnp.where(col, rank_k, rank_out)
    idx_ref[...] = idx_out.astype(jnp.int32)
    wts_ref[...] = wts_out
    rank_ref[...] = rank_out.astype(jnp.int32)
    total = cnt_sc[...] + jnp.sum(hot_sum, axis=0, keepdims=True)
    cnt_sc[...] = total
    cnt_ref[...] = total


def _router(x2d, gain, scale, shift, rw, rb, *, seq, t):
    n_tok, d_model = x2d.shape
    tm, rc = t["rt_tm"], t["norm_rc"]
    tpb = seq // tm
    half = d_model // 2
    kern = functools.partial(_router_kernel, tm=tm, rc=rc, half=half)
    return pl.pallas_call(
        kern,
        grid=(n_tok // tm,),
        in_specs=[
            pl.BlockSpec((tm, d_model), lambda i: (i, 0)),
            pl.BlockSpec((1, d_model), lambda i: (0, 0)),
            pl.BlockSpec((1, 1, d_model), lambda i: (i // tpb, 0, 0)),
            pl.BlockSpec((1, 1, d_model), lambda i: (i // tpb, 0, 0)),
            pl.BlockSpec((d_model, LANES), lambda i: (0, 0)),
            pl.BlockSpec((1, LANES), lambda i: (0, 0)),
        ],
        out_specs=[
            pl.BlockSpec((tm * (half // LANES), LANES), lambda i: (i, 0)),
            pl.BlockSpec((tm, LANES), lambda i: (i, 0)),
            pl.BlockSpec((tm, LANES), lambda i: (i, 0)),
            pl.BlockSpec((tm, LANES), lambda i: (i, 0)),
            pl.BlockSpec((1, LANES), lambda i: (0, 0)),
        ],
        out_shape=[
            jax.ShapeDtypeStruct((n_tok * (half // LANES), LANES), jnp.uint32),
            jax.ShapeDtypeStruct((n_tok, LANES), jnp.int32),
            jax.ShapeDtypeStruct((n_tok, LANES), F32),
            jax.ShapeDtypeStruct((n_tok, LANES), jnp.int32),
            jax.ShapeDtypeStruct((1, LANES), F32),
        ],
        scratch_shapes=[pltpu.VMEM((tm, d_model), BF16), pltpu.VMEM((1, LANES), F32)],
        compiler_params=_cparams("arbitrary"),
        name="norm_router_topk",
    )(x2d, gain, scale, shift, rw, rb)


def _row_token_kernel(dest_ref, zeros_ref, o_ref, sem, *, blk, xp):
    i = pl.program_id(0)

    @pl.when(i == 0)
    def _init():
        fill = pltpu.make_async_copy(zeros_ref, o_ref, sem)
        fill.start()
        fill.wait()

    base = i * blk

    def place(g, carry):
        for u in range(SCALAR_UNROLL):
            a = g * SCALAR_UNROLL + u
            o_ref[dest_ref[a]] = lax.shift_right_logical(base + a, TOP_K.bit_length() - 1) * xp
        return carry
    lax.fori_loop(0, blk // SCALAR_UNROLL, place, 0)


def _row_tokens(dest, n_rows, xp, t):
    n_assign = dest.shape[0]
    blk = min(t["inv_blk"], n_assign)
    assert TOP_K & (TOP_K - 1) == 0 and n_assign % blk == 0 and blk % SCALAR_UNROLL == 0
    return pl.pallas_call(
        functools.partial(_row_token_kernel, blk=blk, xp=xp),
        grid=(n_assign // blk,),
        in_specs=[pl.BlockSpec((blk,), lambda i: (i,), memory_space=pltpu.SMEM),
                  pl.BlockSpec(memory_space=pl.ANY)],
        out_specs=pl.BlockSpec(memory_space=pltpu.SMEM),
        out_shape=jax.ShapeDtypeStruct((n_rows,), jnp.int32),
        scratch_shapes=[pltpu.SemaphoreType.DMA],
        compiler_params=_cparams("arbitrary"),
        name="moe_row_tokens",
    )(dest, jnp.zeros((n_rows,), jnp.int32))


def _expert_kernel(be_ref, na_ref, tok_ref, nxt_ref, src_ref, w1_ref, b1_ref, w2_ref, b2_ref,
                   o_ref, xbuf, sems, *, bm, half, fdim, xp, dp):
    del be_ref
    i = pl.program_id(0)
    n_active = na_ref[0]

    def slab_copy(offs, r, slot):
        src0 = pl.multiple_of(offs[r], xp)
        dst0 = pl.multiple_of(r * dp, SUBLANES)
        return pltpu.make_async_copy(src_ref.at[pl.ds(src0, xp), :],
                                     xbuf.at[slot, pl.ds(dst0, xp), :], sems.at[slot])

    def gather_start(toks, slot):
        def issue(g, carry):
            for u in range(DMA_UNROLL):
                slab_copy(toks, g * DMA_UNROLL + u, slot).start(priority=u % 2)
            return carry
        lax.fori_loop(0, bm // DMA_UNROLL, issue, 0)

    def gather_wait(toks, slot):
        def drain(g, carry):
            for u in range(DMA_UNROLL):
                slab_copy(toks, g * DMA_UNROLL + u, slot).wait()
            return carry
        lax.fori_loop(0, bm // DMA_UNROLL, drain, 0)

    def compute(slot):
        words = jnp.concatenate([xbuf[slot, pl.ds(s, bm, stride=dp), :] for s in range(xp)], axis=1)
        lo = lax.bitcast_convert_type(lax.shift_left(words, jnp.uint32(16)), F32).astype(BF16)
        hi = lax.bitcast_convert_type(words & jnp.uint32(0xFFFF0000), F32).astype(BF16)
        hid = (jnp.dot(lo, w1_ref[0, :half, :], preferred_element_type=F32)
               + jnp.dot(hi, w1_ref[0, half:, :], preferred_element_type=F32)) + b1_ref[0]
        glu = jnp.minimum(hid[:, :fdim], SWIGLU_LIMIT)
        lin = jnp.clip(hid[:, fdim:], -SWIGLU_LIMIT, SWIGLU_LIMIT)
        act = (glu / (1.0 + jnp.exp(-SWIGLU_ALPHA * glu))) * (lin + 1.0)
        y = jnp.dot(act.astype(BF16), w2_ref[0], preferred_element_type=F32) + b2_ref[0]
        bits = lax.bitcast_convert_type(y.astype(BF16).astype(F32), jnp.uint32)
        out_words = lax.shift_right_logical(bits[:, :half], jnp.uint32(16)) | (
            bits[:, half:] & jnp.uint32(0xFFFF0000))
        for c in range(xp):
            o_ref[pl.ds(c, bm, stride=xp), :] = out_words[:, c * LANES:(c + 1) * LANES]

    @pl.when(i == 0)
    def _first():
        gather_start(tok_ref, 0)

    for slot in range(2):
        @pl.when((i % 2 == slot) & (i + 1 < n_active))
        def _prefetch(slot=slot):
            gather_start(nxt_ref, 1 - slot)

        @pl.when((i % 2 == slot) & (i < n_active))
        def _active(slot=slot):
            gather_wait(tok_ref, slot)
            compute(slot)

    @pl.when(i >= n_active)
    def _unused():
        o_ref[...] = jnp.zeros(o_ref.shape, o_ref.dtype)


def _vmem_pitch(rows):
    groups = -(-rows // SUBLANES)
    return (groups + 1 - groups % 2) * SUBLANES


def _experts(block_e, n_active, row_tok, packed, w1, b1, w2, b2, *, bm, xp):
    dp = _vmem_pitch(xp)
    n_blocks = row_tok.shape[0] // bm
    n_exp, d_model, two_f = w1.shape
    fdim, half = two_f // 2, d_model // 2
    grid_spec = pltpu.PrefetchScalarGridSpec(
        num_scalar_prefetch=2,
        grid=(n_blocks,),
        in_specs=[
            pl.BlockSpec((bm,), lambda i, be, na: (i,), memory_space=pltpu.SMEM),
            pl.BlockSpec((bm,), lambda i, be, na: (jnp.minimum(i + 1, n_blocks - 1),),
                         memory_space=pltpu.SMEM),
            pl.BlockSpec(memory_space=pl.ANY),
            pl.BlockSpec((1, d_model, two_f), lambda i, be, na: (be[i], 0, 0)),
            pl.BlockSpec((1, 1, two_f), lambda i, be, na: (be[i], 0, 0)),
            pl.BlockSpec((1, fdim, d_model), lambda i, be, na: (be[i], 0, 0)),
            pl.BlockSpec((1, 1, d_model), lambda i, be, na: (be[i], 0, 0)),
        ],
        out_specs=pl.BlockSpec((bm * xp, LANES), lambda i, be, na: (i, 0)),
        scratch_shapes=[pltpu.VMEM((2, bm * dp, LANES), packed.dtype),
                        pltpu.SemaphoreType.DMA((2,))],
    )
    return pl.pallas_call(
        functools.partial(_expert_kernel, bm=bm, half=half, fdim=fdim, xp=xp, dp=dp),
        grid_spec=grid_spec,
        out_shape=jax.ShapeDtypeStruct((n_blocks * bm * xp, LANES), packed.dtype),
        compiler_params=_cparams("arbitrary"),
        name="moe_experts",
    )(block_e, n_active, row_tok, row_tok, packed, w1, b1.reshape(n_exp, 1, two_f),
      w2, b2.reshape(n_exp, 1, d_model))


def _combine_kernel(dest_ref, nxt_ref, y_ref, wts_ref, x_ref, g_ref, *rest, tm, yp, dp, half, n_steps,
                    next_norm):
    if next_norm:
        ng_ref, nsc_ref, nsh_ref, o_ref, hn_ref, buf, sems = rest
    else:
        o_ref, buf, sems = rest
    i = pl.program_id(0)

    def slab_copy(offs, r, k, slot):
        src0 = pl.multiple_of(offs[r * TOP_K + k], yp)
        dst0 = pl.multiple_of(r * dp, SUBLANES)
        return pltpu.make_async_copy(y_ref.at[pl.ds(src0, yp), :],
                                     buf.at[slot, k, pl.ds(dst0, yp), :], sems.at[slot])

    def gather_start(dests, slot):
        def issue(g, carry):
            for u in range(DMA_UNROLL // TOP_K):
                for k in range(TOP_K):
                    slab_copy(dests, g * (DMA_UNROLL // TOP_K) + u, k, slot).start(priority=k % 2)
            return carry
        lax.fori_loop(0, tm * TOP_K // DMA_UNROLL, issue, 0)

    def gather_wait(dests, slot):
        def drain(g, carry):
            for u in range(DMA_UNROLL // TOP_K):
                for k in range(TOP_K):
                    slab_copy(dests, g * (DMA_UNROLL // TOP_K) + u, k, slot).wait()
            return carry
        lax.fori_loop(0, tm * TOP_K // DMA_UNROLL, drain, 0)

    def compute(slot):
        w = wts_ref[...]
        wk = [jnp.broadcast_to(w[:, k:k + 1], (tm, LANES)) for k in range(TOP_K)]
        for c in range(yp):
            acc_lo = acc_hi = None
            for k in range(TOP_K):
                words = buf[slot, k, pl.ds(c, tm, stride=dp), :]
                lo = wk[k] * lax.bitcast_convert_type(lax.shift_left(words, jnp.uint32(16)), F32)
                hi = wk[k] * lax.bitcast_convert_type(words & jnp.uint32(0xFFFF0000), F32)
                acc_lo = lo if acc_lo is None else acc_lo + lo
                acc_hi = hi if acc_hi is None else acc_hi + hi
            for base, acc in ((0, acc_lo), (half, acc_hi)):
                sl = slice(base + c * LANES, base + (c + 1) * LANES)
                o_ref[:, sl] = x_ref[:, sl] + g_ref[0, :, sl] * acc
        if next_norm:
            x_new = o_ref[...]
            ms = jnp.mean(x_new * x_new, axis=-1, keepdims=True)
            y = x_new * lax.rsqrt(ms + NORM_EPS)
            hn_ref[...] = ((y * ng_ref[...]) * (1.0 + nsc_ref[0]) + nsh_ref[0]).astype(BF16)

    @pl.when(i == 0)
    def _first():
        gather_start(dest_ref, 0)

    for slot in range(2):
        @pl.when((i % 2 == slot) & (i + 1 < n_steps))
        def _prefetch(slot=slot):
            gather_start(nxt_ref, 1 - slot)

        @pl.when(i % 2 == slot)
        def _active(slot=slot):
            gather_wait(dest_ref, slot)
            compute(slot)


def _combine(dest_off, ys, yp, wts, x2d, gate, next_norm, *, seq, t):
    n_tok, d_model = x2d.shape
    tm = t["cmb_tm"]
    dp = _vmem_pitch(yp)
    tpb = seq // tm
    n_steps = n_tok // tm
    row_spec = pl.BlockSpec((tm, d_model), lambda i: (i, 0))
    mod_spec = pl.BlockSpec((1, 1, d_model), lambda i: (i // tpb, 0, 0))
    in_specs = [
        pl.BlockSpec((tm * TOP_K,), lambda i: (i,), memory_space=pltpu.SMEM),
        pl.BlockSpec((tm * TOP_K,), lambda i: (jnp.minimum(i + 1, n_steps - 1),),
                     memory_space=pltpu.SMEM),
        pl.BlockSpec(memory_space=pl.ANY),
        pl.BlockSpec((tm, LANES), lambda i: (i, 0)),
        row_spec,
        mod_spec,
    ]
    out_specs, out_shape = [row_spec], [jax.ShapeDtypeStruct((n_tok, d_model), F32)]
    args = [dest_off, dest_off, ys, wts, x2d, gate]
    if next_norm is not None:
        in_specs += [pl.BlockSpec((1, d_model), lambda i: (0, 0)), mod_spec, mod_spec]
        out_specs.append(row_spec)
        out_shape.append(jax.ShapeDtypeStruct((n_tok, d_model), BF16))
        args += list(next_norm)
    outs = pl.pallas_call(
        functools.partial(_combine_kernel, tm=tm, yp=yp, dp=dp, half=d_model // 2, n_steps=n_steps,
                          next_norm=next_norm is not None),
        grid=(n_steps,),
        in_specs=in_specs,
        out_specs=out_specs,
        out_shape=out_shape,
        scratch_shapes=[pltpu.VMEM((2, TOP_K, tm * dp, LANES), ys.dtype), pltpu.SemaphoreType.DMA((2,))],
        compiler_params=_cparams("arbitrary"),
        name="moe_combine_residual",
    )(*args)
    return (outs[0], outs[1]) if next_norm is not None else (outs[0], None)


def _layer(layer, x2d, h1, mod, next_norm, w_in_all, forget_b, pool_w, pool_scale, q_norm_g,
           k_norm_g, w_out_all, norm2_g, router_w, router_b, w1_f32, b1, w2_f32, b2, *, seq, t):
    n_tok, d_model = x2d.shape
    batch = n_tok // seq
    n_groups, gd, _ = pool_w.shape
    pool_width = n_groups * gd
    att_width = d_model - pool_width
    n_heads = att_width // HEAD_DIM
    n_exp = router_w.shape[1]
    assert pool_width == att_width and n_groups == len(POOL_WINDOWS) and gd % HEAD_DIM == 0
    n_main = pool_width + 3 * att_width
    hp = -(-n_heads // BF16_ROWS) * BF16_ROWS

    shift1, scale1, gate1, shift2, scale2, gate2 = [
        m.reshape(batch, 1, d_model) for m in jnp.split(mod, N_MOD, axis=-1)]

    wf_t = jnp.zeros((hp, d_model), BF16).at[:n_heads].set(w_in_all[layer, n_main:, :])
    fb = jnp.zeros((hp, 1), F32).at[:n_heads, 0].set(forget_b)
    reps = gd // HEAD_DIM
    main, cum = _in_projection(
        h1, w_in_all, layer, n_main, wf_t, fb,
        pool_w.astype(BF16), pool_scale.reshape(n_groups, 1, gd),
        jnp.tile(q_norm_g, reps).reshape(1, gd), jnp.tile(k_norm_g, reps).reshape(1, gd),
        seq=seq, t=t)
    att, w1, w2 = _attention(main, cum, w1_f32, w2_f32, layer, seq=seq, n_heads=n_heads,
                             q_off=pool_width, k_off=pool_width + att_width,
                             v_off=pool_width + 2 * att_width, t=t)
    x1 = _out_projection(main, att, w_out_all, layer, x2d, gate1, seq=seq,
                         pool_width=pool_width, t=t)

    rw = jnp.zeros((d_model, LANES), BF16).at[:, :n_exp].set(router_w.astype(BF16))
    rb = jnp.full((1, LANES), -jnp.inf, F32).at[0, :n_exp].set(router_b)
    packed, idx, wts, rank, counts = _router(
        x1, norm2_g.reshape(1, d_model), scale2, shift2, rw, rb, seq=seq, t=t)

    bm = t["moe_bm"]
    n_assign = n_tok * TOP_K
    n_blocks = -(-n_assign // bm) + n_exp
    counts = counts[0, :n_exp].astype(jnp.int32)
    padded = (counts + bm - 1) // bm * bm
    pad_end = jnp.cumsum(padded)
    pad_start = pad_end - padded
    dest = (pad_start[idx[:, :TOP_K]] + rank[:, :TOP_K]).reshape(-1)
    xp = d_model // 2 // LANES
    row_tok = _row_tokens(dest, n_blocks * bm, xp, t)
    block_start = jnp.arange(n_blocks, dtype=jnp.int32) * bm
    block_e = jnp.minimum(jnp.sum(block_start[:, None] >= pad_end[None, :], axis=1),
                          n_exp - 1).astype(jnp.int32)
    n_active = (pad_end[-1:] // bm).astype(jnp.int32)

    ys = _experts(block_e, n_active, row_tok, packed, w1, b1, w2, b2, bm=bm, xp=xp)
    return _combine(dest * xp, ys, xp, wts, x1, gate2, next_norm, seq=seq, t=t)


def kernel(x, c, ada_w, ada_b, norm1_g, w_in, forget_b, pool_w, pool_scale, q_norm_g, k_norm_g,
           w_out, norm2_g, router_w, router_b, expert_w1, expert_b1, expert_w2, expert_b2):
    batch, seq, d_model = x.shape
    depth = ada_w.shape[0]
    t = _tiles(batch * seq, seq, d_model)
    mod = _modulation(c, ada_w, ada_b, t["mod_tn"])
    x2d = x.reshape(batch * seq, d_model)
    proj_width = w_in.shape[2]
    in_rows = max([r for r in range(BF16_ROWS, t["cast_rows"] * 2 + 1, BF16_ROWS) if proj_width % r == 0]
                  or [proj_width])
    w_in_all = _cast_bf16(jnp.swapaxes(w_in, 1, 2), in_rows)
    w_out_all = _cast_bf16(w_out, t["cast_rows"])

    def first_norm(l):
        shift1, scale1 = jnp.split(mod[l], N_MOD, axis=-1)[:2]
        return (norm1_g[l].reshape(1, d_model), scale1.reshape(batch, 1, d_model),
                shift1.reshape(batch, 1, d_model))

    h1 = _norm_modulate(x2d, *first_norm(0), seq=seq, t=t)
    for l in range(depth):
        next_norm = first_norm(l + 1) if l + 1 < depth else None
        x2d, h1 = _layer(l, x2d, h1, mod[l], next_norm, w_in_all, forget_b[l], pool_w[l],
                         pool_scale[l], q_norm_g[l], k_norm_g[l], w_out_all, norm2_g[l], router_w[l],
                         router_b[l], expert_w1, expert_b1[l], expert_w2, expert_b2[l], seq=seq, t=t)
    return x2d.reshape(batch, seq, d_model)
```
